```python
import math
import jax, jax.numpy as jnp
from jax import lax
import numpy as np

D_MODEL = 2048
BATCH = 4
SEQ = 4096
DEPTH = 2
DEC_BATCH = 2
DEC_SEQ = 8192
PAST_LEN = 128

N_META = 16
POOL_WIDTH = D_MODEL // 2
POOL_WINDOWS = (2, 4, 8, 16)
N_POOL_GROUPS = len(POOL_WINDOWS)
POOL_GROUP = POOL_WIDTH // N_POOL_GROUPS
HYENA_WIDTH = D_MODEL - POOL_WIDTH
HYENA_ORDER = 2
N_HYENA_PROJ = HYENA_ORDER + 1
SHORT_CONV = 3
FILTER_EMB = 33
FILTER_BANDS = (FILTER_EMB - 1) // 2
FILTER_WIDTH = 64
N_DIRS = 2
DECAY_TARGET = 1e-2
FAST_DECAY_PCT = 0.3
SLOW_DECAY_PCT = 1.5
IN_WIDTH = POOL_WIDTH + N_HYENA_PROJ * HYENA_WIDTH
FILTER_OUT = HYENA_ORDER * N_DIRS * HYENA_WIDTH
D_FF = -(-8 * D_MODEL // (3 * 256)) * 256
EPS = 1e-6

kernel_name = 'hybrid_pool_hyena_encoder'


def rmsnorm(x, g):
    x32 = x.astype(jnp.float32)
    y = x32 * lax.rsqrt(jnp.mean(x32 * x32, axis=-1, keepdims=True) + EPS)
    return (y * g.astype(jnp.float32)).astype(x.dtype)


def pool_mixer(u, pool_w, pool_scale):
    B, L, C = u.shape
    u32 = u.astype(jnp.float32)
    cs = jnp.concatenate([jnp.zeros((B, 1, C), jnp.float32), jnp.cumsum(u32, axis=1)], axis=1)
    pos = jnp.arange(L)
    outs = []
    for g, w in enumerate(POOL_WINDOWS):
        lo = jnp.clip(pos - w // 2, 0, L)
        hi = jnp.clip(pos + w // 2, 0, L)
        csg = cs[..., g * POOL_GROUP:(g + 1) * POOL_GROUP]
        cnt = (hi - lo).astype(jnp.float32)[None, :, None]
        mean = (jnp.take(csg, hi, axis=1) - jnp.take(csg, lo, axis=1)) / cnt
        outs.append(mean - u32[..., g * POOL_GROUP:(g + 1) * POOL_GROUP])
    d = jnp.stack(outs, axis=2)
    y = jnp.einsum('blgc,gcd->blgd', d, pool_w.astype(jnp.float32)).reshape(B, L, C)
    return (y * pool_scale.astype(jnp.float32)).astype(u.dtype)


def short_conv(u, w, b):
    up = jnp.pad(u, ((0, 0), (1, 1), (0, 0)))
    return up[:, :-2] * w[0] + up[:, 1:-1] * w[1] + up[:, 2:] * w[2] + b


def hyena_filters(L, w1, b1, w2, b2, w3, b3, freq, w4):
    f32 = jnp.float32
    n = jnp.arange(L, dtype=f32)
    t = n / (L - 1)
    ang = (2.0 * math.pi / L) * n[:, None] * jnp.linspace(1e-4, FILTER_BANDS - 1, FILTER_BANDS, dtype=f32)[None, :]
    z = jnp.concatenate([t[:, None], jnp.cos(ang), -jnp.sin(ang)], axis=-1)
    fr = freq.astype(f32)
    h = jnp.sin(fr * (z @ w1.astype(f32) + b1.astype(f32)))
    h = jnp.sin(fr * (h @ w2.astype(f32) + b2.astype(f32)))
    h = jnp.sin(fr * (h @ w3.astype(f32) + b3.astype(f32)))
    k = (h @ w4.astype(f32)).reshape(L, HYENA_ORDER, N_DIRS, HYENA_WIDTH)
    deltas = jnp.linspace(math.log(DECAY_TARGET) / SLOW_DECAY_PCT,
                          math.log(DECAY_TARGET) / FAST_DECAY_PCT, HYENA_WIDTH, dtype=f32)
    decay = jnp.exp(-t[:, None] * jnp.abs(deltas)[None, :])
    return k * decay[:, None, None, :]


def long_conv(u, kf, kb, skip):
    B, L, C = u.shape
    u32 = u.astype(jnp.float32)
    kfull = jnp.concatenate([kf, jnp.zeros((1, C), jnp.float32), kb[:0:-1]], axis=0)
    U = jnp.fft.rfft(u32, n=2 * L, axis=1)
    K = jnp.fft.rfft(kfull, n=2 * L, axis=0)
    y = jnp.fft.irfft(U * K[None], n=2 * L, axis=1)[:, :L]
    return (y + u32 * skip.astype(jnp.float32)).astype(u.dtype)


def hyena_mixer(u, conv_w, conv_b, w1, b1, w2, b2, w3, b3, freq, w4, skip):
    L = u.shape[1]
    uc = short_conv(u, conv_w, conv_b)
    v, x1, x2 = jnp.split(uc, N_HYENA_PROJ, axis=-1)
    k = hyena_filters(L, w1, b1, w2, b2, w3, b3, freq, w4)
    z = x1 * long_conv(v, k[:, 0, 0], k[:, 0, 1], skip[0])
    return x2 * long_conv(z, k[:, 1, 0], k[:, 1, 1], skip[1])


def trunk(x, meta_tokens, norm1_g, in_proj, pool_w, pool_scale, hy_conv_w, hy_conv_b,
          flt_w1, flt_b1, flt_w2, flt_b2, flt_w3, flt_b3, flt_freq, flt_w4, hy_skip,
          out_proj, norm2_g, w_gate, w_up, w_down, final_g):
    B = x.shape[0]
    meta = jnp.broadcast_to(meta_tokens[None].astype(x.dtype), (B, N_META, D_MODEL))
    h = jnp.concatenate([meta, x], axis=1)
    for l in range(DEPTH):
        hn = rmsnorm(h, norm1_g[l])
        proj = hn @ in_proj[l]
        a = pool_mixer(proj[..., :POOL_WIDTH], pool_w[l], pool_scale[l])
        b = hyena_mixer(proj[..., POOL_WIDTH:], hy_conv_w[l], hy_conv_b[l],
                        flt_w1[l], flt_b1[l], flt_w2[l], flt_b2[l], flt_w3[l], flt_b3[l],
                        flt_freq[l], flt_w4[l], hy_skip[l])
        h = h + jnp.concatenate([a, b], axis=-1) @ out_proj[l]
        hn = rmsnorm(h, norm2_g[l])
        h = h + (jax.nn.silu(hn @ w_gate[l]) * (hn @ w_up[l])) @ w_down[l]
    return rmsnorm(h, final_g)[:, N_META:]


def setup_inputs(seed: int = 0) -> dict:
    key = jax.random.key(seed)
    ks = jax.random.split(key, 32)
    f32 = jnp.float32
    nrm = lambda k, shape, s: jax.random.normal(k, shape, f32) * s
    return {
        'x_prompt': nrm(ks[0], (BATCH, SEQ, D_MODEL), 1.0),
        'x_sample': nrm(ks[1], (DEC_BATCH, DEC_SEQ, D_MODEL), 1.0),
        'meta_tokens': nrm(ks[2], (N_META, D_MODEL), 1.0),
        'norm1_g': 1.0 + nrm(ks[3], (DEPTH, D_MODEL), 0.02),
        'in_proj': nrm(ks[4], (DEPTH, D_MODEL, IN_WIDTH), D_MODEL ** -0.5),
        'pool_w': nrm(ks[5], (DEPTH, N_POOL_GROUPS, POOL_GROUP, POOL_GROUP), POOL_GROUP ** -0.5),
        'pool_scale': 1.0 + nrm(ks[6], (DEPTH, POOL_WIDTH), 0.02),
        'hy_conv_w': nrm(ks[7], (DEPTH, SHORT_CONV, N_HYENA_PROJ * HYENA_WIDTH), SHORT_CONV ** -0.5),
        'hy_conv_b': nrm(ks[8], (DEPTH, N_HYENA_PROJ * HYENA_WIDTH), 0.02),
        'flt_w1': nrm(ks[9], (DEPTH, FILTER_EMB, FILTER_WIDTH), FILTER_EMB ** -0.5),
        'flt_b1': nrm(ks[10], (DEPTH, FILTER_WIDTH), 0.02),
        'flt_w2': nrm(ks[11], (DEPTH, FILTER_WIDTH, FILTER_WIDTH), FILTER_WIDTH ** -0.5),
        'flt_b2': nrm(ks[12], (DEPTH, FILTER_WIDTH), 0.02),
        'flt_w3': nrm(ks[13], (DEPTH, FILTER_WIDTH, FILTER_WIDTH), FILTER_WIDTH ** -0.5),
        'flt_b3': nrm(ks[14], (DEPTH, FILTER_WIDTH), 0.02),
        'flt_freq': 1.0 + nrm(ks[15], (DEPTH, FILTER_WIDTH), 0.02),
        'flt_w4': nrm(ks[16], (DEPTH, FILTER_WIDTH, FILTER_OUT), 0.05 * FILTER_WIDTH ** -0.5),
        'hy_skip': nrm(ks[17], (DEPTH, HYENA_ORDER, HYENA_WIDTH), 0.1),
        'out_proj': nrm(ks[18], (DEPTH, D_MODEL, D_MODEL), D_MODEL ** -0.5),
        'norm2_g': 1.0 + nrm(ks[19], (DEPTH, D_MODEL), 0.02),
        'w_gate': nrm(ks[20], (DEPTH, D_MODEL, D_FF), D_MODEL ** -0.5),
        'w_up': nrm(ks[21], (DEPTH, D_MODEL, D_FF), D_MODEL ** -0.5),
        'w_down': nrm(ks[22], (DEPTH, D_FF, D_MODEL), D_FF ** -0.5),
        'final_g': 1.0 + nrm(ks[23], (D_MODEL,), 0.02),
    }


def reference(x_prompt, x_sample, meta_tokens, norm1_g, in_proj, pool_w, pool_scale,
              hy_conv_w, hy_conv_b, flt_w1, flt_b1, flt_w2, flt_b2, flt_w3, flt_b3,
              flt_freq, flt_w4, hy_skip, out_proj, norm2_g, w_gate, w_up, w_down, final_g):
    y_prompt = trunk(x_prompt, meta_tokens, norm1_g, in_proj, pool_w, pool_scale, hy_conv_w, hy_conv_b,
                     flt_w1, flt_b1, flt_w2, flt_b2, flt_w3, flt_b3, flt_freq, flt_w4, hy_skip,
                     out_proj, norm2_g, w_gate, w_up, w_down, final_g)
    y_sample = trunk(x_sample, meta_tokens, norm1_g, in_proj, pool_w, pool_scale, hy_conv_w, hy_conv_b,
                     flt_w1, flt_b1, flt_w2, flt_b2, flt_w3, flt_b3, flt_freq, flt_w4, hy_skip,
                     out_proj, norm2_g, w_gate, w_up, w_down, final_g)
    return (y_prompt, y_sample)
```

```python
import functools
import math

import numpy as np
import jax
import jax.numpy as jnp
from jax import lax
from jax.experimental import pallas as pl
from jax.experimental.pallas import tpu as pltpu

F32 = jnp.float32
BF16 = jnp.bfloat16

EPS = 1e-6
N_META = 16
POOL_WINDOWS = (2, 4, 8, 16)
POOL_GROUP = 256
FILTER_BANDS = 16
DECAY_TARGET = 1e-2
FAST_DECAY_PCT = 0.3
SLOW_DECAY_PCT = 1.5

HALO = 8
FFT_N2 = 80
TOKEN_TILE = 512
VMEM_LIMIT = 56 * 1024 * 1024


def _cparams(n_axes):
    return pltpu.CompilerParams(
        dimension_semantics=("arbitrary",) * n_axes,
        vmem_limit_bytes=VMEM_LIMIT)


def _rmsnorm_rows(x, g):
    ms = jnp.mean(x * x, axis=-1, keepdims=True)
    return x * lax.rsqrt(ms + EPS) * g


def _norm_matmul_kernel(x_ref, g_ref, w_ref, o_ref, xn_ref):
    @pl.when(pl.program_id(1) == 0)
    def _():
        xn_ref[...] = _rmsnorm_rows(x_ref[...], g_ref[...]).astype(BF16)

    o_ref[...] = jnp.dot(xn_ref[...], w_ref[...],
                         preferred_element_type=F32).astype(o_ref.dtype)


def norm_matmul(x, g, w, *, tn, out_dtype):
    T, D = x.shape
    n_out = w.shape[1]
    tm = TOKEN_TILE
    return pl.pallas_call(
        _norm_matmul_kernel,
        grid=(pl.cdiv(T, tm), n_out // tn),
        in_specs=[pl.BlockSpec((tm, D), lambda i, j: (i, 0)),
                  pl.BlockSpec((1, D), lambda i, j: (0, 0)),
                  pl.BlockSpec((D, tn), lambda i, j: (0, j))],
        out_specs=pl.BlockSpec((tm, tn), lambda i, j: (i, j)),
        out_shape=jax.ShapeDtypeStruct((T, n_out), out_dtype),
        scratch_shapes=[pltpu.VMEM((tm, D), BF16)],
        compiler_params=_cparams(2),
        name="norm_matmul",
    )(x, g.reshape(1, D), w)


def _norm_swiglu_kernel(x_ref, g_ref, wg_ref, wu_ref, o_ref, xn_ref):
    @pl.when(pl.program_id(1) == 0)
    def _():
        xn_ref[...] = _rmsnorm_rows(x_ref[...], g_ref[...]).astype(BF16)

    xn = xn_ref[...]
    gate = jnp.dot(xn, wg_ref[...], preferred_element_type=F32)
    up = jnp.dot(xn, wu_ref[...], preferred_element_type=F32)
    o_ref[...] = (gate * jax.nn.sigmoid(gate) * up).astype(o_ref.dtype)


def norm_swiglu(x, g, wg, wu, *, tn):
    T, D = x.shape
    n_out = wg.shape[1]
    tm = TOKEN_TILE
    return pl.pallas_call(
        _norm_swiglu_kernel,
        grid=(pl.cdiv(T, tm), n_out // tn),
        in_specs=[pl.BlockSpec((tm, D), lambda i, j: (i, 0)),
                  pl.BlockSpec((1, D), lambda i, j: (0, 0)),
                  pl.BlockSpec((D, tn), lambda i, j: (0, j)),
                  pl.BlockSpec((D, tn), lambda i, j: (0, j))],
        out_specs=pl.BlockSpec((tm, tn), lambda i, j: (i, j)),
        out_shape=jax.ShapeDtypeStruct((T, n_out), BF16),
        scratch_shapes=[pltpu.VMEM((tm, D), BF16)],
        compiler_params=_cparams(2),
        name="norm_swiglu",
    )(x, g.reshape(1, D), wg, wu)


def _matmul_residual_kernel(*refs, n_in):
    xs, ws = refs[:n_in], refs[n_in:2 * n_in]
    r_ref, o_ref = refs[2 * n_in], refs[2 * n_in + 1]
    acc = r_ref[...]
    for x_ref, w_ref in zip(xs, ws):
        acc = acc + jnp.dot(x_ref[...], w_ref[...], preferred_element_type=F32)
    o_ref[...] = acc


def matmul_residual(xs, w, res, *, tn):
    T, n_out = res.shape
    tm = TOKEN_TILE
    n_in = len(xs)
    x_specs, w_specs = [], []
    row0 = 0
    for x in xs:
        k = x.shape[1]
        assert row0 % k == 0
        x_specs.append(pl.BlockSpec((tm, k), lambda i, j: (i, 0)))
        w_specs.append(pl.BlockSpec((k, tn), functools.partial(
            lambda i, j, rb: (rb, j), rb=row0 // k)))
        row0 += k
    return pl.pallas_call(
        functools.partial(_matmul_residual_kernel, n_in=n_in),
        grid=(pl.cdiv(T, tm), n_out // tn),
        in_specs=x_specs + w_specs + [pl.BlockSpec((tm, tn), lambda i, j: (i, j))],
        out_specs=pl.BlockSpec((tm, tn), lambda i, j: (i, j)),
        out_shape=jax.ShapeDtypeStruct((T, n_out), F32),
        compiler_params=_cparams(2),
        name="matmul_residual",
    )(*xs, *([w] * n_in), res)


def _final_norm_kernel(x_ref, g_ref, o_ref):
    o_ref[...] = _rmsnorm_rows(x_ref[...], g_ref[...])


def final_norm(x, g):
    T, D = x.shape
    tm = TOKEN_TILE
    return pl.pallas_call(
        _final_norm_kernel,
        grid=(pl.cdiv(T, tm),),
        in_specs=[pl.BlockSpec((tm, D), lambda i: (i, 0)),
                  pl.BlockSpec((1, D), lambda i: (0, 0))],
        out_specs=pl.BlockSpec((tm, D), lambda i: (i, 0)),
        out_shape=jax.ShapeDtypeStruct((T, D), F32),
        compiler_params=_cparams(1),
        name="final_norm",
    )(x, g.reshape(1, D))


def _fill_padded(pad_ref, x_ref, L):
    width = pad_ref.shape[1]
    pad_ref[0:HALO, :] = jnp.zeros((HALO, width), F32)
    pad_ref[HALO + L:HALO + L + HALO, :] = jnp.zeros((HALO, width), F32)
    pad_ref[HALO:HALO + L, :] = x_ref[...]


def _chunk_start(k, rows, L):
    return pl.multiple_of(jnp.minimum(k * rows, L - rows), 8)


POOL_CHUNK = 512


def _pool_kernel(u_ref, w_ref, s_ref, o_ref, pad_ref, *, L):
    g = pl.program_id(1)
    _fill_padded(pad_ref, u_ref, L)
    rows = POOL_CHUNK
    n_chunks = pl.cdiv(L, rows)

    def run(window):
        hw = window // 2

        def chunk(k, carry):
            start = _chunk_start(k, rows, L)
            win = pad_ref[pl.ds(start, rows + 2 * HALO), :]
            total = win[HALO - hw:HALO - hw + rows]
            for i in range(1, window):
                total = total + win[HALO - hw + i:HALO - hw + i + rows]
            t = start + lax.broadcasted_iota(jnp.int32, (rows, POOL_GROUP), 0)
            cnt = (jnp.minimum(t + hw, L) - jnp.maximum(t - hw, 0)).astype(F32)
            d = total / cnt - win[HALO:HALO + rows]
            y = jnp.dot(d.astype(BF16), w_ref[...], preferred_element_type=F32)
            o_ref[pl.ds(start, rows), :] = (y * s_ref[...]).astype(o_ref.dtype)
            return carry

        lax.fori_loop(0, n_chunks, chunk, 0)

    for gi, window in enumerate(POOL_WINDOWS):
        pl.when(g == gi)(functools.partial(run, window))


def pool_mixer(proj, pool_w, pool_scale, *, B, L):
    n_groups = len(POOL_WINDOWS)
    width = n_groups * POOL_GROUP
    return pl.pallas_call(
        functools.partial(_pool_kernel, L=L),
        grid=(B, n_groups),
        in_specs=[pl.BlockSpec((None, L, POOL_GROUP), lambda b, g: (b, 0, g)),
                  pl.BlockSpec((None, POOL_GROUP, POOL_GROUP), lambda b, g: (g, 0, 0)),
                  pl.BlockSpec((1, POOL_GROUP), lambda b, g: (0, g))],
        out_specs=pl.BlockSpec((None, L, POOL_GROUP), lambda b, g: (b, 0, g)),
        out_shape=jax.ShapeDtypeStruct((B, L, width), BF16),
        scratch_shapes=[pltpu.VMEM((L + 2 * HALO, POOL_GROUP), F32)],
        compiler_params=_cparams(2),
        name="pool_mixer",
    )(proj, pool_w, pool_scale.reshape(1, width))


PREP_CHUNK = 512
PREP_COLS = 128


def _prep_kernel(u_ref, w_ref, b_ref, o_ref, pad_ref, *, L, N):
    _fill_padded(pad_ref, u_ref, L)
    rows = PREP_CHUNK
    n_chunks = pl.cdiv(L, rows)
    w0, w1, w2 = w_ref[0:1, :], w_ref[1:2, :], w_ref[2:3, :]
    bias = b_ref[...]

    def chunk(k, carry):
        start = _chunk_start(k, rows, L)
        win = pad_ref[pl.ds(start, rows + 2 * HALO), :]
        y = (win[HALO - 1:HALO - 1 + rows] * w0 + win[HALO:HALO + rows] * w1
             + win[HALO + 1:HALO + 1 + rows] * w2 + bias)
        o_ref[pl.ds(start, rows), :] = y.astype(o_ref.dtype)
        return carry

    lax.fori_loop(0, n_chunks, chunk, 0)
    o_ref[L:N, :] = jnp.zeros((N - L, o_ref.shape[1]), o_ref.dtype)


def hyena_prep(proj, conv_w, conv_b, *, B, L, N, col0, width):
    cb = PREP_COLS
    n_cb = width // cb
    assert col0 % cb == 0
    return pl.pallas_call(
        functools.partial(_prep_kernel, L=L, N=N),
        grid=(3, B, n_cb),
        in_specs=[pl.BlockSpec((None, L, cb), lambda s, b, c: (b, 0, col0 // cb + s * n_cb + c)),
                  pl.BlockSpec((3, cb), lambda s, b, c: (0, s * n_cb + c)),
                  pl.BlockSpec((1, cb), lambda s, b, c: (0, s * n_cb + c))],
        out_specs=pl.BlockSpec((None, None, N, cb), lambda s, b, c: (s, b, 0, c)),
        out_shape=jax.ShapeDtypeStruct((3, B, N, width), BF16),
        scratch_shapes=[pltpu.VMEM((L + 2 * HALO, cb), F32)],
        compiler_params=_cparams(3),
        name="hyena_prep",
    )(proj, conv_w, conv_b.reshape(1, 3 * width))


FILTER_ROWS = 320
HI = lax.Precision.HIGHEST


def _filter_kernel(bands_ref, deltas_ref, w1t_ref, w1c_ref, w1s_ref, b1_ref, w2_ref, b2_ref,
                   w3_ref, b3_ref, fr_ref, w4_ref, skip_ref, o_ref, *, L, N, C):
    rows = FILTER_ROWS
    m = pl.program_id(0) * rows + lax.broadcasted_iota(jnp.int32, (rows, 1), 0)
    fwd = m < L
    bwd = m > N - L
    pos = jnp.where(fwd, m, N - m).astype(F32)
    t = pos / (L - 1)
    ang = ((2.0 * math.pi / L) * pos) * bands_ref[...]
    fr = fr_ref[...]
    pre = (t * w1t_ref[...]
           + jnp.dot(jnp.cos(ang), w1c_ref[...], precision=HI, preferred_element_type=F32)
           + jnp.dot(-jnp.sin(ang), w1s_ref[...], precision=HI, preferred_element_type=F32)
           + b1_ref[...])
    h = jnp.sin(fr * pre)
    h = jnp.sin(fr * (jnp.dot(h, w2_ref[...], precision=HI, preferred_element_type=F32) + b2_ref[...]))
    h = jnp.sin(fr * (jnp.dot(h, w3_ref[...], precision=HI, preferred_element_type=F32) + b3_ref[...]))
    decay = jnp.exp(-t * deltas_ref[...])
    for order in range(2):
        kf = jnp.dot(h, w4_ref[:, (2 * order) * C:(2 * order + 1) * C],
                     precision=HI, preferred_element_type=F32)
        kb = jnp.dot(h, w4_ref[:, (2 * order + 1) * C:(2 * order + 2) * C],
                     precision=HI, preferred_element_type=F32)
        k = jnp.where(fwd, kf, jnp.where(bwd, kb, 0.0)) * decay
        k = jnp.where(m == 0, k + skip_ref[order:order + 1, :], k)
        o_ref[order] = k.astype(o_ref.dtype)


def hyena_filter_taps(w1, b1, w2, b2, w3, b3, freq, w4, skip, *, L, N):
    C = skip.shape[1]
    fw = w2.shape[0]
    bands = jnp.linspace(1e-4, FILTER_BANDS - 1, FILTER_BANDS, dtype=F32).reshape(1, FILTER_BANDS)
    deltas = jnp.abs(jnp.linspace(math.log(DECAY_TARGET) / SLOW_DECAY_PCT,
                                  math.log(DECAY_TARGET) / FAST_DECAY_PCT, C, dtype=F32)).reshape(1, C)
    small = lambda a: pl.BlockSpec(a.shape, lambda i: (0,) * a.ndim)
    args = (bands, deltas, w1[0:1], w1[1:1 + FILTER_BANDS], w1[1 + FILTER_BANDS:],
            b1.reshape(1, fw), w2, b2.reshape(1, fw), w3, b3.reshape(1, fw),
            freq.reshape(1, fw), w4, skip)
    return pl.pallas_call(
        functools.partial(_filter_kernel, L=L, N=N, C=C),
        grid=(N // FILTER_ROWS,),
        in_specs=[small(a) for a in args],
        out_specs=pl.BlockSpec((2, FILTER_ROWS, C), lambda i: (0, i, 0)),
        out_shape=jax.ShapeDtypeStruct((2, N, C), BF16),
        compiler_params=_cparams(1),
        name="hyena_filter_taps",
    )(*args)


def _real_form(m):
    return np.block([[m.real, -m.imag], [m.imag, m.real]])


@functools.lru_cache(maxsize=None)
def _dft_tables(N1, N2):
    N = N1 * N2
    k1 = np.arange(N1)
    g1 = np.exp(-2j * np.pi * ((k1[:, None] * k1[None, :]) % N1) / N1)
    stage1 = _real_form(g1)
    stage1_real = np.concatenate([g1.real, g1.imag], 0)
    stage1_inv = _real_form(np.conj(g1))
    n2 = np.arange(N2)
    k = k1[:, None, None] + N1 * n2[None, :, None]
    mk = np.exp(-2j * np.pi * ((k * n2[None, None, :]) % N) / N)
    stage2 = np.stack([_real_form(mk[i]) for i in range(N1)])
    stage2_inv = np.stack([_real_form(np.conj(mk[i]).T) for i in range(N1)])
    return tuple(a.astype(BF16) for a in (stage1, stage1_real, stage1_inv, stage2, stage2_inv))


STAGE1_COLS = 4096


def _left_matmul_kernel(m_ref, x_ref, o_ref):
    o_ref[...] = jnp.dot(m_ref[...], x_ref[...],
                         preferred_element_type=F32).astype(o_ref.dtype)


def dft_stage1(mat, x):
    P, K, cols = x.shape
    M = mat.shape[0]
    cb = STAGE1_COLS
    return pl.pallas_call(
        _left_matmul_kernel,
        grid=(P, cols // cb),
        in_specs=[pl.BlockSpec((M, K), lambda p, c: (0, 0)),
                  pl.BlockSpec((None, K, cb), lambda p, c: (p, 0, c))],
        out_specs=pl.BlockSpec((None, M, cb), lambda p, c: (p, 0, c)),
        out_shape=jax.ShapeDtypeStruct((P, M, cols), BF16),
        compiler_params=_cparams(2),
        name="dft_stage1",
    )(mat, x)


def _left_matmul_gate_kernel(m_ref, x_ref, gate_ref, o_ref):
    y = jnp.dot(m_ref[...], x_ref[...], preferred_element_type=F32)
    o_ref[...] = (y * gate_ref[...].astype(F32)).astype(o_ref.dtype)


def dft_stage1_inverse_gate(mat, x, gate):
    P, K, cols = x.shape
    M = mat.shape[0]
    cb = STAGE1_COLS
    return pl.pallas_call(
        _left_matmul_gate_kernel,
        grid=(P, cols // cb),
        in_specs=[pl.BlockSpec((M, K), lambda p, c: (0, 0)),
                  pl.BlockSpec((None, K, cb), lambda p, c: (p, 0, c)),
                  pl.BlockSpec((None, M, cb), lambda p, c: (p, 0, c))],
        out_specs=pl.BlockSpec((None, M, cb), lambda p, c: (p, 0, c)),
        out_shape=jax.ShapeDtypeStruct((P, M, cols), BF16),
        compiler_params=_cparams(2),
        name="dft_stage1_inverse_gate",
    )(mat, x, gate)


STAGE2_K1 = 8


def _stage2_spectrum_kernel(m_ref, a_ref, o_ref, *, scale):
    n2 = a_ref.shape[2]
    for i in range(STAGE2_K1):
        a = jnp.concatenate([a_ref[0, i], a_ref[1, i]], axis=0)
        x = jnp.dot(m_ref[i], a, preferred_element_type=F32) * scale
        o_ref[0, i] = x[:n2]
        o_ref[1, i] = x[n2:]


def filter_spectrum(stage2, a, *, N1, N2, C):
    kb = STAGE2_K1
    return pl.pallas_call(
        functools.partial(_stage2_spectrum_kernel, scale=1.0 / (N1 * N2)),
        grid=(2, N1 // kb),
        in_specs=[pl.BlockSpec((kb, 2 * N2, 2 * N2), lambda o, k: (k, 0, 0)),
                  pl.BlockSpec((None, 2, kb, N2, C), lambda o, k: (o, 0, k, 0, 0))],
        out_specs=pl.BlockSpec((None, 2, kb, N2, C), lambda o, k: (o, 0, k, 0, 0)),
        out_shape=jax.ShapeDtypeStruct((2, 2, N1, N2, C), F32),
        compiler_params=_cparams(2),
        name="filter_spectrum",
    )(stage2, a)


def _stage2_conv_kernel(mf_ref, mi_ref, a_ref, k_ref, o_ref):
    n2 = a_ref.shape[2]
    for i in range(STAGE2_K1):
        a = jnp.concatenate([a_ref[0, i], a_ref[1, i]], axis=0)
        x = jnp.dot(mf_ref[i], a, preferred_element_type=F32)
        xr, xi = x[:n2], x[n2:]
        kr, ki = k_ref[0, i], k_ref[1, i]
        y = jnp.concatenate([xr * kr - xi * ki, xr * ki + xi * kr], axis=0).astype(BF16)
        b = jnp.dot(mi_ref[i], y, preferred_element_type=F32).astype(o_ref.dtype)
        o_ref[0, i] = b[:n2]
        o_ref[1, i] = b[n2:]


def dft_stage2_conv(stage2, stage2_inv, a, kspec, order, *, N1, N2, C):
    P = a.shape[0]
    kb = STAGE2_K1
    return pl.pallas_call(
        _stage2_conv_kernel,
        grid=(P, N1 // kb),
        in_specs=[pl.BlockSpec((kb, 2 * N2, 2 * N2), lambda p, k: (k, 0, 0)),
                  pl.BlockSpec((kb, 2 * N2, 2 * N2), lambda p, k: (k, 0, 0)),
                  pl.BlockSpec((None, 2, kb, N2, C), lambda p, k: (p, 0, k, 0, 0)),
                  pl.BlockSpec((None, 2, kb, N2, C), lambda p, k: (order, 0, k, 0, 0))],
        out_specs=pl.BlockSpec((None, 2, kb, N2, C), lambda p, k: (p, 0, k, 0, 0)),
        out_shape=jax.ShapeDtypeStruct((P, 2, N1, N2, C), BF16),
        compiler_params=_cparams(2),
        name="dft_stage2_conv",
    )(stage2, stage2_inv, a, kspec)


def _fft_factor(L):
    n1 = -(-(2 * L - 1) // FFT_N2)
    return -(-n1 // STAGE2_K1) * STAGE2_K1


def hyena_mixer(proj, conv_w, conv_b, w1, b1, w2, b2, w3, b3, freq, w4, skip, *, B, L, col0):
    C = skip.shape[1]
    N1, N2 = _fft_factor(L), FFT_N2
    N = N1 * N2
    P = B // 2
    stage1, stage1_real, stage1_inv, stage2, stage2_inv = _dft_tables(N1, N2)

    taps = hyena_filter_taps(w1, b1, w2, b2, w3, b3, freq, w4, skip, L=L, N=N)
    ka = dft_stage1(stage1_real, taps.reshape(2, N1, N2 * C))
    kspec = filter_spectrum(stage2, ka.reshape(2, 2, N1, N2, C), N1=N1, N2=N2, C=C)

    streams = hyena_prep(proj, conv_w, conv_b, B=B, L=L, N=N, col0=col0, width=C)
    as_pairs = lambda a: a.reshape(P, 2 * N1, N2 * C)
    z = as_pairs(streams[0])
    for order in range(2):
        a = dft_stage1(stage1, z)
        b = dft_stage2_conv(stage2, stage2_inv, a.reshape(P, 2, N1, N2, C), kspec, order,
                            N1=N1, N2=N2, C=C)
        z = dft_stage1_inverse_gate(stage1_inv, b.reshape(P, 2 * N1, N2 * C),
                                    as_pairs(streams[1 + order]))
    return z.reshape(B, N, C)[:, :L].reshape(B * L, C)


def _trunk(x, meta_tokens, norm1_g, in_proj, pool_w, pool_scale, hy_conv_w, hy_conv_b,
           flt_w1, flt_b1, flt_w2, flt_b2, flt_w3, flt_b3, flt_freq, flt_w4, hy_skip,
           out_proj, norm2_g, w_gate, w_up, w_down, final_g):
    B, S, D = x.shape
    L = S + N_META
    T = B * L
    depth = in_proj.shape[0]
    pool_width = pool_scale.shape[1]
    meta = jnp.broadcast_to(meta_tokens[None].astype(x.dtype), (B, N_META, D))
    h = jnp.concatenate([meta, x], axis=1).reshape(T, D)
    for l in range(depth):
        proj = norm_matmul(h, norm1_g[l], in_proj[l], tn=512, out_dtype=F32)
        proj = proj.reshape(B, L, proj.shape[1])
        a = pool_mixer(proj, pool_w[l], pool_scale[l], B=B, L=L).reshape(T, pool_width)
        b = hyena_mixer(proj, hy_conv_w[l], hy_conv_b[l], flt_w1[l], flt_b1[l], flt_w2[l],
                        flt_b2[l], flt_w3[l], flt_b3[l], flt_freq[l], flt_w4[l], hy_skip[l],
                        B=B, L=L, col0=pool_width)
        h = matmul_residual([a, b], out_proj[l], h, tn=512)
        act = norm_swiglu(h, norm2_g[l], w_gate[l], w_up[l], tn=512)
        h = matmul_residual([act], w_down[l], h, tn=512)
    y = final_norm(h, final_g)
    return y.reshape(B, L, D)[:, N_META:]


def kernel(x_prompt, x_sample, meta_tokens, norm1_g, in_proj, pool_w, pool_scale, hy_conv_w, hy_conv_b, flt_w1, flt_b1, flt_w2, flt_b2, flt_w3, flt_b3, flt_freq, flt_w4, hy_skip, out_proj, norm2_g, w_gate, w_up, w_down, final_g):
    params = (meta_tokens, norm1_g, in_proj.astype(BF16), pool_w.astype(BF16), pool_scale,
              hy_conv_w, hy_conv_b, flt_w1, flt_b1, flt_w2, flt_b2, flt_w3, flt_b3, flt_freq,
              flt_w4, hy_skip, out_proj.astype(BF16), norm2_g, w_gate.astype(BF16),
              w_up.astype(BF16), w_down.astype(BF16), final_g)
    return (_trunk(x_prompt, *params), _trunk(x_sample, *params))
```

```python
import functools
import math

import numpy as np
import jax
import jax.numpy as jnp
from jax import lax
from jax.experimental import pallas as pl
from jax.experimental.pallas import tpu as pltpu

F32 = jnp.float32
BF16 = jnp.bfloat16
U32 = jnp.uint32

EPS = 1e-6
N_META = 16
POOL_WINDOWS = (2, 4, 8, 16)
POOL_GROUP = 256
FILTER_BANDS = 16
DECAY_TARGET = 1e-2
FAST_DECAY_PCT = 0.3
SLOW_DECAY_PCT = 1.5

LANES = 128
HALO = 8
FFT_N2 = 80
ROWS_BIG = 1040
ROWS_SMALL = 272
ROWS_FINAL = 256
VMEM_LIMIT = 56 * 1024 * 1024
HI = lax.Precision.HIGHEST


def _cparams(n_axes):
    return pltpu.CompilerParams(
        dimension_semantics=("arbitrary",) * n_axes,
        vmem_limit_bytes=VMEM_LIMIT)


def _resident(shape):
    return pl.BlockSpec(shape, lambda *_: (0,) * len(shape), pipeline_mode=pl.Buffered(1))


def _unpack(word):
    lo = lax.bitcast_convert_type(word << 16, F32)
    hi = lax.bitcast_convert_type(word & jnp.uint32(0xFFFF0000), F32)
    return lo, hi


def _pack(lo, hi):
    lo = lax.bitcast_convert_type(lo.astype(BF16).astype(F32), U32)
    hi = lax.bitcast_convert_type(hi.astype(BF16).astype(F32), U32)
    return hi | (lo >> 16)


def _lane_chunks(x):
    return [x[:, c * LANES:(c + 1) * LANES] for c in range(x.shape[1] // LANES)]


def _rmsnorm_rows(x, g):
    ms = jnp.mean(x * x, axis=-1, keepdims=True)
    return x * lax.rsqrt(ms + EPS) * g


def _norm_kernel(x_ref, g_ref, o_ref):
    o_ref[...] = _rmsnorm_rows(x_ref[...], g_ref[...]).astype(o_ref.dtype)


def norm_rows(x, g):
    T, D = x.shape
    tm = ROWS_BIG
    return pl.pallas_call(
        _norm_kernel,
        grid=(pl.cdiv(T, tm),),
        in_specs=[pl.BlockSpec((tm, D), lambda i: (i, 0)),
                  pl.BlockSpec((1, D), lambda i: (0, 0))],
        out_specs=pl.BlockSpec((tm, D), lambda i: (i, 0)),
        out_shape=jax.ShapeDtypeStruct((T, D), BF16),
        compiler_params=_cparams(1),
        name="norm_rows",
    )(x, g.reshape(1, D))


MATMUL_COLS = 1024


def _matmul_kernel(x_ref, w_ref, o_ref):
    x = x_ref[...]
    for c in range(0, o_ref.shape[1], MATMUL_COLS):
        o_ref[:, c:c + MATMUL_COLS] = jnp.dot(
            x, w_ref[:, c:c + MATMUL_COLS], preferred_element_type=F32).astype(o_ref.dtype)


def matmul_resident(x, w):
    T, K = x.shape
    n_out = w.shape[1]
    tm = ROWS_BIG
    return pl.pallas_call(
        _matmul_kernel,
        grid=(pl.cdiv(T, tm),),
        in_specs=[pl.BlockSpec((tm, K), lambda i: (i, 0)), _resident((K, n_out))],
        out_specs=pl.BlockSpec((tm, n_out), lambda i: (i, 0)),
        out_shape=jax.ShapeDtypeStruct((T, n_out), BF16),
        compiler_params=_cparams(1),
        name="matmul_resident",
    )(x, w)


def _swiglu_kernel(x_ref, wg_ref, wu_ref, o_ref):
    x = x_ref[...]
    gate = jnp.dot(x, wg_ref[...], preferred_element_type=F32)
    up = jnp.dot(x, wu_ref[...], preferred_element_type=F32)
    o_ref[...] = (gate * jax.nn.sigmoid(gate) * up).astype(o_ref.dtype)


def swiglu(x, wg, wu, *, tn):
    T, K = x.shape
    n_out = wg.shape[1]
    tm = ROWS_BIG
    return pl.pallas_call(
        _swiglu_kernel,
        grid=(n_out // tn, pl.cdiv(T, tm)),
        in_specs=[pl.BlockSpec((tm, K), lambda j, i: (i, 0)),
                  pl.BlockSpec((K, tn), lambda j, i: (0, j)),
                  pl.BlockSpec((K, tn), lambda j, i: (0, j))],
        out_specs=pl.BlockSpec((tm, tn), lambda j, i: (i, j)),
        out_shape=jax.ShapeDtypeStruct((T, n_out), BF16),
        compiler_params=_cparams(2),
        name="swiglu",
    )(x, wg, wu)


def _mixer_residual_norm_kernel(a_ref, b_ref, w_ref, r_ref, g_ref, h_ref, xn_ref):
    ka = a_ref.shape[1]
    b = jnp.concatenate([b_ref[c] for c in range(b_ref.shape[0])], axis=1).astype(BF16)
    acc = (r_ref[...] + jnp.dot(a_ref[...], w_ref[0:ka, :], preferred_element_type=F32)
           + jnp.dot(b, w_ref[ka:, :], preferred_element_type=F32))
    h_ref[...] = acc
    xn_ref[...] = _rmsnorm_rows(acc, g_ref[...]).astype(xn_ref.dtype)


def mixer_residual_norm(a, b, w, res, g, *, B, L):
    D = res.shape[1]
    ka = a.shape[2]
    chunks = b.shape[1]
    tm = ROWS_SMALL
    h, xn = pl.pallas_call(
        _mixer_residual_norm_kernel,
        grid=(B, pl.cdiv(L, tm)),
        in_specs=[pl.BlockSpec((None, tm, ka), lambda s, i: (s, i, 0)),
                  pl.BlockSpec((None, chunks, tm, LANES), lambda s, i: (s, 0, i, 0)),
                  _resident(w.shape),
                  pl.BlockSpec((None, tm, D), lambda s, i: (s, i, 0)),
                  pl.BlockSpec((1, D), lambda s, i: (0, 0))],
        out_specs=[pl.BlockSpec((None, tm, D), lambda s, i: (s, i, 0)),
                   pl.BlockSpec((None, tm, D), lambda s, i: (s, i, 0))],
        out_shape=[jax.ShapeDtypeStruct((B, L, D), F32), jax.ShapeDtypeStruct((B, L, D), BF16)],
        compiler_params=_cparams(2),
        name="mixer_residual_norm",
    )(a, b, w, res.reshape(B, L, D), g.reshape(1, D))
    return h.reshape(B * L, D), xn.reshape(B * L, D)


def _residual_norm_kernel(x_ref, w_ref, r_ref, g_ref, h_ref, xn_ref):
    acc = r_ref[...] + jnp.dot(x_ref[...], w_ref[...], preferred_element_type=F32)
    h_ref[...] = acc
    xn_ref[...] = _rmsnorm_rows(acc, g_ref[...]).astype(xn_ref.dtype)


def matmul_residual_norm(x, w, res, g):
    T, D = res.shape
    tm = ROWS_SMALL
    return pl.pallas_call(
        _residual_norm_kernel,
        grid=(pl.cdiv(T, tm),),
        in_specs=[pl.BlockSpec((tm, x.shape[1]), lambda i: (i, 0)),
                  _resident(w.shape),
                  pl.BlockSpec((tm, D), lambda i: (i, 0)),
                  pl.BlockSpec((1, D), lambda i: (0, 0))],
        out_specs=[pl.BlockSpec((tm, D), lambda i: (i, 0)),
                   pl.BlockSpec((tm, D), lambda i: (i, 0))],
        out_shape=[jax.ShapeDtypeStruct((T, D), F32), jax.ShapeDtypeStruct((T, D), BF16)],
        compiler_params=_cparams(1),
        name="matmul_residual_norm",
    )(x, w, res, g.reshape(1, D))


def _residual_final_kernel(x_ref, w_ref, r_ref, g_ref, y_ref):
    acc = r_ref[...] + jnp.dot(x_ref[...], w_ref[...], preferred_element_type=F32)
    y_ref[...] = _rmsnorm_rows(acc, g_ref[...])


def matmul_residual_final(x, w, res, g, *, B, L, skip):
    K = x.shape[1]
    D = res.shape[1]
    S = L - skip
    tm = ROWS_FINAL
    assert L % 16 == 0 and skip % 16 == 0 and tm % 16 == 0
    rows = lambda b, i: pl.multiple_of(b * L + skip + i * tm, 16)
    return pl.pallas_call(
        _residual_final_kernel,
        grid=(B, S // tm),
        in_specs=[pl.BlockSpec((pl.Element(tm), pl.Element(K)), lambda b, i: (rows(b, i), 0)),
                  _resident(w.shape),
                  pl.BlockSpec((pl.Element(tm), pl.Element(D)), lambda b, i: (rows(b, i), 0)),
                  pl.BlockSpec((1, D), lambda b, i: (0, 0))],
        out_specs=pl.BlockSpec((None, tm, D), lambda b, i: (b, i, 0)),
        out_shape=jax.ShapeDtypeStruct((B, S, D), F32),
        compiler_params=_cparams(2),
        name="matmul_residual_final",
    )(x, w, res, g.reshape(1, D))


def _fill_padded(pad_ref, x_ref, L):
    rows, width = pad_ref.shape
    pad_ref[0:HALO, :] = jnp.zeros((HALO, width), F32)
    pad_ref[HALO + L:rows, :] = jnp.zeros((rows - HALO - L, width), F32)
    pad_ref[HALO:HALO + L, :] = x_ref[...].astype(F32)


def _chunk_start(k, rows, L):
    return pl.multiple_of(jnp.minimum(k * rows, L - rows), 16)


POOL_CHUNK = 512


def _pool_kernel(u_ref, w_ref, s_ref, o_ref, pad_ref, *, L):
    g = pl.program_id(1)
    _fill_padded(pad_ref, u_ref, L)
    rows = POOL_CHUNK
    n_chunks = pl.cdiv(L, rows)

    def run(window):
        hw = window // 2

        def chunk(k, carry):
            start = _chunk_start(k, rows, L)
            win = pad_ref[pl.ds(start, rows + 2 * HALO), :]
            total = win[HALO - hw:HALO - hw + rows]
            for i in range(1, window):
                total = total + win[HALO - hw + i:HALO - hw + i + rows]
            t = start + lax.broadcasted_iota(jnp.int32, (rows, POOL_GROUP), 0)
            cnt = (jnp.minimum(t + hw, L) - jnp.maximum(t - hw, 0)).astype(F32)
            d = total / cnt - win[HALO:HALO + rows]
            y = jnp.dot(d.astype(BF16), w_ref[...], preferred_element_type=F32)
            o_ref[pl.ds(start, rows), :] = (y * s_ref[...]).astype(o_ref.dtype)
            return carry

        lax.fori_loop(0, n_chunks, chunk, 0)

    for gi, window in enumerate(POOL_WINDOWS):
        pl.when(g == gi)(functools.partial(run, window))


def pool_mixer(proj, pool_w, pool_scale, *, B, L):
    n_groups = len(POOL_WINDOWS)
    width = n_groups * POOL_GROUP
    return pl.pallas_call(
        functools.partial(_pool_kernel, L=L),
        grid=(B, n_groups),
        in_specs=[pl.BlockSpec((None, L, POOL_GROUP), lambda b, g: (b, 0, g)),
                  pl.BlockSpec((None, POOL_GROUP, POOL_GROUP), lambda b, g: (g, 0, 0)),
                  pl.BlockSpec((1, POOL_GROUP), lambda b, g: (0, g))],
        out_specs=pl.BlockSpec((None, L, POOL_GROUP), lambda b, g: (b, 0, g)),
        out_shape=jax.ShapeDtypeStruct((B, L, width), BF16),
        scratch_shapes=[pltpu.VMEM((L + 2 * HALO, POOL_GROUP), F32)],
        compiler_params=_cparams(2),
        name="pool_mixer",
    )(proj, pool_w, pool_scale.reshape(1, width))


def _fft_factor(L):
    n1 = -(-(2 * L - 1) // FFT_N2)
    return -(-n1 // STAGE2_K1) * STAGE2_K1


def _valid_n1(L):
    return -(-(-(-L // FFT_N2)) // 8) * 8


def _prep_kernel(u0_ref, u1_ref, w_ref, b_ref, o_ref, pad0_ref, pad1_ref, *, L, n1v):
    n2_count, n1, _ = o_ref.shape
    _fill_padded(pad0_ref, u0_ref, L)
    _fill_padded(pad1_ref, u1_ref, L)
    w0, w1, w2 = w_ref[0:1, :], w_ref[1:2, :], w_ref[2:3, :]
    bias = b_ref[...]
    pos0 = lax.broadcasted_iota(jnp.int32, (n1v, LANES), 0) * n2_count

    def per_n2(n2, carry):
        def conv(pad_ref):
            tap = lambda d: pad_ref[pl.ds(HALO + d + n2, n1v, stride=n2_count), :]
            y = tap(-1) * w0 + tap(0) * w1 + tap(1) * w2 + bias
            return jnp.where(pos0 + n2 < L, y, 0.0)

        o_ref[n2, 0:n1v, :] = _pack(conv(pad0_ref), conv(pad1_ref))
        return carry

    lax.fori_loop(0, n2_count, per_n2, 0)
    o_ref[:, n1v:n1, :] = jnp.zeros((n2_count, n1 - n1v, LANES), o_ref.dtype)


def hyena_prep(proj, conv_w, conv_b, *, B, L, N1, col0, width):
    N2 = FFT_N2
    n_cb = width // LANES
    n1v = _valid_n1(L)
    assert col0 % LANES == 0 and n1v <= N1
    cols = lambda s, c: col0 // LANES + s * n_cb + c
    pad = pltpu.VMEM((n1v * N2 + 2 * HALO, LANES), F32)
    return pl.pallas_call(
        functools.partial(_prep_kernel, L=L, n1v=n1v),
        grid=(3, B // 2, n_cb),
        in_specs=[pl.BlockSpec((None, L, LANES), lambda s, p, c: (2 * p, 0, cols(s, c))),
                  pl.BlockSpec((None, L, LANES), lambda s, p, c: (2 * p + 1, 0, cols(s, c))),
                  pl.BlockSpec((3, LANES), lambda s, p, c: (0, s * n_cb + c)),
                  pl.BlockSpec((1, LANES), lambda s, p, c: (0, s * n_cb + c))],
        out_specs=pl.BlockSpec((None, None, None, N2, N1, LANES),
                               lambda s, p, c: (s, p, c, 0, 0, 0)),
        out_shape=jax.ShapeDtypeStruct((3, B // 2, n_cb, N2, N1, LANES), U32),
        scratch_shapes=[pad, pad],
        compiler_params=_cparams(3),
        name="hyena_prep",
    )(proj, proj, conv_w, conv_b.reshape(1, 3 * width))


def _unpack_kernel(x_ref, o_ref):
    n2_count, n1v, _ = x_ref.shape

    def per_n2(n2, carry):
        lo, hi = _unpack(x_ref[n2])
        o_ref[0, pl.ds(n2, n1v, stride=n2_count), :] = lo
        o_ref[1, pl.ds(n2, n1v, stride=n2_count), :] = hi
        return carry

    lax.fori_loop(0, n2_count, per_n2, 0)


def unpack_pairs(z, *, L):
    P, chunks, N2, N1, _ = z.shape
    n1v = _valid_n1(L)
    out = pl.pallas_call(
        _unpack_kernel,
        grid=(P, chunks),
        in_specs=[pl.BlockSpec((None, None, N2, n1v, LANES), lambda p, c: (p, c, 0, 0, 0))],
        out_specs=pl.BlockSpec((None, 2, None, n1v * N2, LANES), lambda p, c: (p, 0, c, 0, 0)),
        out_shape=jax.ShapeDtypeStruct((P, 2, chunks, n1v * N2, LANES), F32),
        compiler_params=_cparams(2),
        name="unpack_pairs",
    )(z)
    return out.reshape(2 * P, chunks, n1v * N2, LANES)


FILTER_N2 = 4


def _filter_kernel(bands_ref, deltas_ref, w1t_ref, w1c_ref, w1s_ref, b1_ref, w2_ref, b2_ref,
                   w3_ref, b3_ref, fr_ref, w4_ref, skip_ref, o_ref, *, L, N, C):
    n1 = o_ref.shape[3]
    rows = FILTER_N2 * n1
    n2_base = pl.program_id(0) * FILTER_N2
    dot_hi = functools.partial(jnp.dot, precision=HI, preferred_element_type=F32)

    def cyclic_index(shape, axis):
        r = lax.broadcasted_iota(jnp.int32, shape, axis)
        j = sum((r >= k * n1).astype(jnp.int32) for k in range(1, FILTER_N2))
        return (r - j * n1) * FFT_N2 + n2_base + j

    m_row = cyclic_index((1, rows), 1)
    pos_row = jnp.where(m_row < L, m_row, N - m_row).astype(F32)
    t_row = pos_row / (L - 1)
    ang = bands_ref[...] * ((2.0 * math.pi / L) * pos_row)
    fr = fr_ref[...]
    pre = (w1t_ref[...] * t_row + dot_hi(w1c_ref[...], jnp.cos(ang))
           + dot_hi(w1s_ref[...], -jnp.sin(ang)) + b1_ref[...])
    h = jnp.sin(fr * pre)
    h = jnp.sin(fr * (dot_hi(w2_ref[...], h) + b2_ref[...]))
    h = jnp.sin(fr * (dot_hi(w3_ref[...], h) + b3_ref[...]))

    m = cyclic_index((rows, 1), 0)
    fwd = m < L
    bwd = m > N - L
    t = jnp.where(fwd, m, N - m).astype(F32) / (L - 1)
    decay = jnp.exp(-t * deltas_ref[...])
    taps_of = lambda col: lax.dot_general(
        h, w4_ref[:, col * C:(col + 1) * C], (((0,), (0,)), ((), ())),
        preferred_element_type=F32)
    for order in range(2):
        k = jnp.where(fwd, taps_of(2 * order), jnp.where(bwd, taps_of(2 * order + 1), 0.0)) * decay
        k = jnp.where(m == 0, k + skip_ref[order:order + 1, :], k)
        for c, kc in enumerate(_lane_chunks(k)):
            for j in range(FILTER_N2):
                o_ref[order, c, j] = kc[j * n1:(j + 1) * n1]


def hyena_filter_taps(w1, b1, w2, b2, w3, b3, freq, w4, skip, *, L, N1):
    C = skip.shape[1]
    fw = w2.shape[0]
    N2 = FFT_N2
    bands = jnp.linspace(1e-4, FILTER_BANDS - 1, FILTER_BANDS, dtype=F32).reshape(FILTER_BANDS, 1)
    deltas = jnp.abs(jnp.linspace(math.log(DECAY_TARGET) / SLOW_DECAY_PCT,
                                  math.log(DECAY_TARGET) / FAST_DECAY_PCT, C, dtype=F32)).reshape(1, C)
    col = lambda a: a.reshape(fw, 1)
    args = (bands, deltas, w1[0:1].T, w1[1:1 + FILTER_BANDS].T, w1[1 + FILTER_BANDS:].T,
            col(b1), w2.T, col(b2), w3.T, col(b3), col(freq), w4, skip)
    small = lambda a: pl.BlockSpec(a.shape, lambda i: (0,) * a.ndim)
    return pl.pallas_call(
        functools.partial(_filter_kernel, L=L, N=N1 * N2, C=C),
        grid=(N2 // FILTER_N2,),
        in_specs=[small(a) for a in args],
        out_specs=pl.BlockSpec((2, C // LANES, FILTER_N2, N1, LANES), lambda i: (0, 0, i, 0, 0)),
        out_shape=jax.ShapeDtypeStruct((2, C // LANES, N2, N1, LANES), F32),
        compiler_params=_cparams(1),
        name="hyena_filter_taps",
    )(*args)


def _real_form(m):
    return np.block([[m.real, -m.imag], [m.imag, m.real]])


@functools.lru_cache(maxsize=None)
def _dft_tables(N1, N2):
    N = N1 * N2
    k1 = np.arange(N1)
    g1 = np.exp(-2j * np.pi * ((k1[:, None] * k1[None, :]) % N1) / N1)
    split = lambda m: np.stack([m[:, :N1], m[:, N1:]])
    stage1 = split(_real_form(g1))
    stage1_inv = split(_real_form(np.conj(g1)))
    n2 = np.arange(N2)
    k = k1[:, None, None] + N1 * n2[None, :, None]
    mk = np.exp(-2j * np.pi * ((k * n2[None, None, :]) % N) / N)
    stage2 = np.stack([_real_form(mk[i]) for i in range(N1)])
    stage2_inv = np.stack([_real_form(np.conj(mk[i]).T) for i in range(N1)])
    return tuple(a.astype(BF16) for a in (stage1, stage1_inv, stage2, stage2_inv))


STAGE1_J = 8
STAGE1_CHUNKS = 4


STAGE2_K1 = 8


def _time_spec(jb, cb, n1, lead):
    if lead is None:
        return pl.BlockSpec((None, cb, jb, n1, LANES), lambda p, j, c: (p, c, j, 0, 0))
    return pl.BlockSpec((None, None, cb, jb, n1, LANES), lambda p, j, c: (lead, p, c, j, 0, 0))


def _freq_spec_n2(jb, cb, n1):
    kb = STAGE2_K1
    return pl.BlockSpec((None, cb, n1 // kb, jb * kb, LANES), lambda p, j, c: (p, c, 0, j, 0))


def _load_time(x_ref, j):
    return jnp.concatenate([x_ref[c, j] for c in range(x_ref.shape[0])], axis=1)


def _load_freq(x_ref, j):
    kb = STAGE2_K1
    rows = slice(j * kb, (j + 1) * kb)
    return jnp.concatenate(
        [jnp.concatenate([x_ref[c, k, rows, :] for k in range(x_ref.shape[1])], axis=0)
         for c in range(x_ref.shape[0])], axis=1)


def _store_time(o_ref, j, word):
    for c, w in enumerate(_lane_chunks(word)):
        o_ref[c, j] = w


def _store_freq(o_ref, j, word):
    kb = STAGE2_K1
    for c, w in enumerate(_lane_chunks(word)):
        for k in range(o_ref.shape[1]):
            o_ref[c, k, j * kb:(j + 1) * kb, :] = w[k * kb:(k + 1) * kb]


def _stage1_matmul(m_ref, word, real_input):
    if real_input:
        return jnp.dot(m_ref[0], word.astype(BF16), preferred_element_type=F32)
    re, im = _unpack(word)
    return (jnp.dot(m_ref[0], re.astype(BF16), preferred_element_type=F32)
            + jnp.dot(m_ref[1], im.astype(BF16), preferred_element_type=F32))


def _stage1_kernel(m_ref, x_ref, o_ref, *, real_input):
    n1 = x_ref.shape[2]
    for j in range(STAGE1_J):
        y = _stage1_matmul(m_ref, _load_time(x_ref, j), real_input)
        _store_freq(o_ref, j, _pack(y[:n1], y[n1:]))


def dft_stage1(mat, x, lead=None):
    P, chunks, N2, N1, _ = x.shape[-5:]
    jb, cb, kb = STAGE1_J, STAGE1_CHUNKS, STAGE2_K1
    return pl.pallas_call(
        functools.partial(_stage1_kernel, real_input=(x.dtype == F32)),
        grid=(P, N2 // jb, chunks // cb),
        in_specs=[pl.BlockSpec(mat.shape, lambda p, j, c: (0, 0, 0)),
                  _time_spec(jb, cb, N1, lead)],
        out_specs=_freq_spec_n2(jb, cb, N1),
        out_shape=jax.ShapeDtypeStruct((P, chunks, N1 // kb, N2 * kb, LANES), U32),
        compiler_params=_cparams(3),
        name="dft_stage1",
    )(mat, x)


def _stage1_gate_kernel(m_ref, x_ref, gate_ref, o_ref):
    n1 = o_ref.shape[2]
    for j in range(STAGE1_J):
        y = _stage1_matmul(m_ref, _load_freq(x_ref, j), False)
        g_lo, g_hi = _unpack(_load_time(gate_ref, j))
        _store_time(o_ref, j, _pack(y[:n1] * g_lo, y[n1:] * g_hi))


def dft_stage1_inverse_gate(mat, x, gates, lead):
    P, chunks, _, _, _ = x.shape
    N2, N1 = gates.shape[-3], gates.shape[-2]
    jb, cb = STAGE1_J, STAGE1_CHUNKS
    return pl.pallas_call(
        _stage1_gate_kernel,
        grid=(P, N2 // jb, chunks // cb),
        in_specs=[pl.BlockSpec(mat.shape, lambda p, j, c: (0, 0, 0)),
                  _freq_spec_n2(jb, cb, N1),
                  _time_spec(jb, cb, N1, lead)],
        out_specs=_time_spec(jb, cb, N1, None),
        out_shape=jax.ShapeDtypeStruct((P, chunks, N2, N1, LANES), U32),
        compiler_params=_cparams(3),
        name="dft_stage1_inverse_gate",
    )(mat, x, gates)


def _freq_spec_k1(chunks, n2):
    kb = STAGE2_K1
    return pl.BlockSpec((None, chunks, None, n2 * kb, LANES), lambda p, k: (p, 0, k, 0, 0))


def _stage2_rows(ref, i):
    return pl.ds(i, ref.shape[1] // STAGE2_K1, stride=STAGE2_K1)


def _stage2_operand(a_ref, i):
    rows = _stage2_rows(a_ref, i)
    word = jnp.concatenate([a_ref[c, rows, :] for c in range(a_ref.shape[0])], axis=1)
    re, im = _unpack(word)
    return jnp.concatenate([re, im], axis=0).astype(BF16)


def _stage2_spectrum_kernel(m_ref, a_ref, o_ref, *, scale):
    n2 = o_ref.shape[2]
    for i in range(STAGE2_K1):
        x = jnp.dot(m_ref[i], _stage2_operand(a_ref, i), preferred_element_type=F32) * scale
        o_ref[0, i] = x[:n2]
        o_ref[1, i] = x[n2:]


def filter_spectrum(stage2, a, *, N2):
    kb = STAGE2_K1
    _, chunks, n_kb, _, _ = a.shape
    N1 = n_kb * kb
    C = chunks * LANES
    return pl.pallas_call(
        functools.partial(_stage2_spectrum_kernel, scale=1.0 / (N1 * N2)),
        grid=(2, n_kb),
        in_specs=[pl.BlockSpec((kb, 2 * N2, 2 * N2), lambda o, k: (k, 0, 0)),
                  _freq_spec_k1(chunks, N2)],
        out_specs=pl.BlockSpec((None, 2, kb, N2, C), lambda o, k: (o, 0, k, 0, 0)),
        out_shape=jax.ShapeDtypeStruct((2, 2, N1, N2, C), F32),
        compiler_params=_cparams(2),
        name="filter_spectrum",
    )(stage2, a)


def _stage2_conv_kernel(mf_ref, mi_ref, a_ref, k_ref, o_ref):
    n2 = k_ref.shape[2]
    for i in range(STAGE2_K1):
        x = jnp.dot(mf_ref[i], _stage2_operand(a_ref, i), preferred_element_type=F32)
        xr, xi = x[:n2], x[n2:]
        kr, ki = k_ref[0, i], k_ref[1, i]
        y = jnp.concatenate([xr * kr - xi * ki, xr * ki + xi * kr], axis=0).astype(BF16)
        b = jnp.dot(mi_ref[i], y, preferred_element_type=F32)
        rows = _stage2_rows(o_ref, i)
        for c, word in enumerate(_lane_chunks(_pack(b[:n2], b[n2:]))):
            o_ref[c, rows, :] = word


def dft_stage2_conv(stage2, stage2_inv, a, kspec, order):
    kb = STAGE2_K1
    P, chunks, n_kb, _, _ = a.shape
    N2, C = kspec.shape[3], kspec.shape[4]
    return pl.pallas_call(
        _stage2_conv_kernel,
        grid=(P, n_kb),
        in_specs=[pl.BlockSpec((kb, 2 * N2, 2 * N2), lambda p, k: (k, 0, 0)),
                  pl.BlockSpec((kb, 2 * N2, 2 * N2), lambda p, k: (k, 0, 0)),
                  _freq_spec_k1(chunks, N2),
                  pl.BlockSpec((None, 2, kb, N2, C), lambda p, k: (order, 0, k, 0, 0))],
        out_specs=_freq_spec_k1(chunks, N2),
        out_shape=jax.ShapeDtypeStruct(a.shape, U32),
        compiler_params=_cparams(2),
        name="dft_stage2_conv",
    )(stage2, stage2_inv, a, kspec)


def hyena_mixer(proj, conv_w, conv_b, w1, b1, w2, b2, w3, b3, freq, w4, skip, *, B, L, col0):
    C = skip.shape[1]
    N1, N2 = _fft_factor(L), FFT_N2
    stage1, stage1_inv, stage2, stage2_inv = _dft_tables(N1, N2)

    taps = hyena_filter_taps(w1, b1, w2, b2, w3, b3, freq, w4, skip, L=L, N1=N1)
    kspec = filter_spectrum(stage2, dft_stage1(stage1, taps), N2=N2)

    streams = hyena_prep(proj, conv_w, conv_b, B=B, L=L, N1=N1, col0=col0, width=C)
    a = dft_stage1(stage1, streams, lead=0)
    for order in range(2):
        b = dft_stage2_conv(stage2, stage2_inv, a, kspec, order)
        z = dft_stage1_inverse_gate(stage1_inv, b, streams, 1 + order)
        if order == 0:
            a = dft_stage1(stage1, z)
    return unpack_pairs(z, L=L)


def _trunk(x, meta_tokens, norm1_g, in_proj, pool_w, pool_scale, hy_conv_w, hy_conv_b,
           flt_w1, flt_b1, flt_w2, flt_b2, flt_w3, flt_b3, flt_freq, flt_w4, hy_skip,
           out_proj, norm2_g, w_gate, w_up, w_down, final_g):
    B, S, D = x.shape
    L = S + N_META
    T = B * L
    depth = in_proj.shape[0]
    pool_width = pool_scale.shape[1]
    meta = jnp.broadcast_to(meta_tokens[None].astype(x.dtype), (B, N_META, D))
    h = jnp.concatenate([meta, x], axis=1).reshape(T, D)
    xn = norm_rows(h, norm1_g[0])
    for l in range(depth):
        proj = matmul_resident(xn, in_proj[l]).reshape(B, L, in_proj.shape[2])
        a = pool_mixer(proj, pool_w[l], pool_scale[l], B=B, L=L)
        b = hyena_mixer(proj, hy_conv_w[l], hy_conv_b[l], flt_w1[l], flt_b1[l], flt_w2[l],
                        flt_b2[l], flt_w3[l], flt_b3[l], flt_freq[l], flt_w4[l], hy_skip[l],
                        B=B, L=L, col0=pool_width)
        h, xn = mixer_residual_norm(a, b, out_proj[l], h, norm2_g[l], B=B, L=L)
        act = swiglu(xn, w_gate[l], w_up[l], tn=512)
        if l + 1 < depth:
            h, xn = matmul_residual_norm(act, w_down[l], h, norm1_g[l + 1])
    return matmul_residual_final(act, w_down[depth - 1], h, final_g, B=B, L=L, skip=N_META)


def kernel(x_prompt, x_sample, meta_tokens, norm1_g, in_proj, pool_w, pool_scale, hy_conv_w, hy_conv_b, flt_w1, flt_b1, flt_w2, flt_b2, flt_w3, flt_b3, flt_freq, flt_w4, hy_skip, out_proj, norm2_g, w_gate, w_up, w_down, final_g):
    params = (meta_tokens, norm1_g, in_proj.astype(BF16), pool_w.astype(BF16), pool_scale,
              hy_conv_w, hy_conv_b, flt_w1, flt_b1, flt_w2, flt_b2, flt_w3, flt_b3, flt_freq,
              flt_w4, hy_skip, out_proj.astype(BF16), norm2_g, w_gate.astype(BF16),
              w_up.astype(BF16), w_down.astype(BF16), final_g)
    return (_trunk(x_prompt, *params), _trunk(x_sample, *params))
```

```python
import functools
import math

import numpy as np
import jax
import jax.numpy as jnp
from jax import lax
from jax.experimental import pallas as pl
from jax.experimental.pallas import tpu as pltpu

F32 = jnp.float32
BF16 = jnp.bfloat16
U32 = jnp.uint32

EPS = 1e-6
N_META = 16
POOL_WINDOWS = (2, 4, 8, 16)
POOL_GROUP = 256
FILTER_BANDS = 16
DECAY_TARGET = 1e-2
FAST_DECAY_PCT = 0.3
SLOW_DECAY_PCT = 1.5

LANES = 128
HALO = 8
FFT_N2 = 80
ROWS_BIG = 1040
ROWS_SMALL = 272
ROWS_FINAL = 256
VMEM_LIMIT = 56 * 1024 * 1024
HI = lax.Precision.HIGHEST


def _cparams(n_axes):
    return pltpu.CompilerParams(
        dimension_semantics=("arbitrary",) * n_axes,
        vmem_limit_bytes=VMEM_LIMIT)


def _resident_layer(w, layer):
    return pl.BlockSpec((None,) + w.shape[1:], lambda *_: (layer, 0, 0),
                        pipeline_mode=pl.Buffered(1))


def _unpack(word):
    lo = lax.bitcast_convert_type(word << 16, F32)
    hi = lax.bitcast_convert_type(word & jnp.uint32(0xFFFF0000), F32)
    return lo, hi


def _pack(lo, hi):
    lo = lax.bitcast_convert_type(lo.astype(BF16).astype(F32), U32)
    hi = lax.bitcast_convert_type(hi.astype(BF16).astype(F32), U32)
    return hi | (lo >> 16)


def _lane_chunks(x):
    return [x[:, c * LANES:(c + 1) * LANES] for c in range(x.shape[1] // LANES)]


def _rmsnorm_rows(x, g):
    ms = jnp.mean(x * x, axis=-1, keepdims=True)
    return x * lax.rsqrt(ms + EPS) * g


def _norm_kernel(x_ref, g_ref, o_ref):
    o_ref[...] = _rmsnorm_rows(x_ref[...], g_ref[...]).astype(o_ref.dtype)


def norm_rows(x, g):
    T, D = x.shape
    tm = ROWS_BIG
    return pl.pallas_call(
        _norm_kernel,
        grid=(pl.cdiv(T, tm),),
        in_specs=[pl.BlockSpec((tm, D), lambda i: (i, 0)),
                  pl.BlockSpec((1, D), lambda i: (0, 0))],
        out_specs=pl.BlockSpec((tm, D), lambda i: (i, 0)),
        out_shape=jax.ShapeDtypeStruct((T, D), BF16),
        compiler_params=_cparams(1),
        name="norm_rows",
    )(x, g.reshape(1, D))


MATMUL_COLS = 1024


def _matmul_kernel(x_ref, w_ref, o_ref):
    x = x_ref[...]
    for c in range(0, o_ref.shape[1], MATMUL_COLS):
        o_ref[:, c:c + MATMUL_COLS] = jnp.dot(
            x, w_ref[:, c:c + MATMUL_COLS], preferred_element_type=F32).astype(o_ref.dtype)


def matmul_resident(x, w, layer):
    T, K = x.shape
    n_out = w.shape[2]
    tm = ROWS_BIG
    return pl.pallas_call(
        _matmul_kernel,
        grid=(pl.cdiv(T, tm),),
        in_specs=[pl.BlockSpec((tm, K), lambda i: (i, 0)), _resident_layer(w, layer)],
        out_specs=pl.BlockSpec((tm, n_out), lambda i: (i, 0)),
        out_shape=jax.ShapeDtypeStruct((T, n_out), BF16),
        compiler_params=_cparams(1),
        name="matmul_resident",
    )(x, w)


def _swiglu_kernel(x_ref, wg_ref, wu_ref, o_ref):
    x = x_ref[...]
    gate = jnp.dot(x, wg_ref[...], preferred_element_type=F32)
    up = jnp.dot(x, wu_ref[...], preferred_element_type=F32)
    o_ref[...] = (gate * jax.nn.sigmoid(gate) * up).astype(o_ref.dtype)


def swiglu(x, wg, wu, layer, *, tn):
    T, K = x.shape
    n_out = wg.shape[2]
    tm = ROWS_BIG
    return pl.pallas_call(
        _swiglu_kernel,
        grid=(n_out // tn, pl.cdiv(T, tm)),
        in_specs=[pl.BlockSpec((tm, K), lambda j, i: (i, 0)),
                  pl.BlockSpec((None, K, tn), lambda j, i: (layer, 0, j)),
                  pl.BlockSpec((None, K, tn), lambda j, i: (layer, 0, j))],
        out_specs=pl.BlockSpec((tm, tn), lambda j, i: (i, j)),
        out_shape=jax.ShapeDtypeStruct((T, n_out), BF16),
        compiler_params=_cparams(2),
        name="swiglu",
    )(x, wg, wu)


def _mixer_residual_norm_kernel(a_ref, b_ref, w_ref, r_ref, g_ref, h_ref, xn_ref):
    ka = a_ref.shape[1]
    b = jnp.concatenate([b_ref[c] for c in range(b_ref.shape[0])], axis=1).astype(BF16)
    acc = (r_ref[...] + jnp.dot(a_ref[...], w_ref[0:ka, :], preferred_element_type=F32)
           + jnp.dot(b, w_ref[ka:, :], preferred_element_type=F32))
    h_ref[...] = acc
    xn_ref[...] = _rmsnorm_rows(acc, g_ref[...]).astype(xn_ref.dtype)


def mixer_residual_norm(a, b, w, layer, res, g, *, B, L):
    D = res.shape[1]
    ka = a.shape[2]
    chunks = b.shape[1]
    tm = ROWS_SMALL
    h, xn = pl.pallas_call(
        _mixer_residual_norm_kernel,
        grid=(B, pl.cdiv(L, tm)),
        in_specs=[pl.BlockSpec((None, tm, ka), lambda s, i: (s, i, 0)),
                  pl.BlockSpec((None, chunks, tm, LANES), lambda s, i: (s, 0, i, 0)),
                  _resident_layer(w, layer),
                  pl.BlockSpec((None, tm, D), lambda s, i: (s, i, 0)),
                  pl.BlockSpec((1, D), lambda s, i: (0, 0))],
        out_specs=[pl.BlockSpec((None, tm, D), lambda s, i: (s, i, 0)),
                   pl.BlockSpec((None, tm, D), lambda s, i: (s, i, 0))],
        out_shape=[jax.ShapeDtypeStruct((B, L, D), F32), jax.ShapeDtypeStruct((B, L, D), BF16)],
        compiler_params=_cparams(2),
        name="mixer_residual_norm",
    )(a, b, w, res.reshape(B, L, D), g.reshape(1, D))
    return h.reshape(B * L, D), xn.reshape(B * L, D)


def _residual_norm_kernel(x_ref, w_ref, r_ref, g_ref, h_ref, xn_ref):
    acc = r_ref[...] + jnp.dot(x_ref[...], w_ref[...], preferred_element_type=F32)
    h_ref[...] = acc
    xn_ref[...] = _rmsnorm_rows(acc, g_ref[...]).astype(xn_ref.dtype)


def matmul_residual_norm(x, w, layer, res, g):
    T, D = res.shape
    tm = ROWS_SMALL
    return pl.pallas_call(
        _residual_norm_kernel,
        grid=(pl.cdiv(T, tm),),
        in_specs=[pl.BlockSpec((tm, x.shape[1]), lambda i: (i, 0)),
                  _resident_layer(w, layer),
                  pl.BlockSpec((tm, D), lambda i: (i, 0)),
                  pl.BlockSpec((1, D), lambda i: (0, 0))],
        out_specs=[pl.BlockSpec((tm, D), lambda i: (i, 0)),
                   pl.BlockSpec((tm, D), lambda i: (i, 0))],
        out_shape=[jax.ShapeDtypeStruct((T, D), F32), jax.ShapeDtypeStruct((T, D), BF16)],
        compiler_params=_cparams(1),
        name="matmul_residual_norm",
    )(x, w, res, g.reshape(1, D))


def _residual_final_kernel(x_ref, w_ref, r_ref, g_ref, y_ref):
    acc = r_ref[...] + jnp.dot(x_ref[...], w_ref[...], preferred_element_type=F32)
    y_ref[...] = _rmsnorm_rows(acc, g_ref[...])


def matmul_residual_final(x, w, layer, res, g, *, B, L, skip):
    K = x.shape[1]
    D = res.shape[1]
    S = L - skip
    tm = ROWS_FINAL
    assert L % 16 == 0 and skip % 16 == 0 and tm % 16 == 0
    rows = lambda b, i: pl.multiple_of(b * L + skip + i * tm, 16)
    return pl.pallas_call(
        _residual_final_kernel,
        grid=(B, S // tm),
        in_specs=[pl.BlockSpec((pl.Element(tm), pl.Element(K)), lambda b, i: (rows(b, i), 0)),
                  _resident_layer(w, layer),
                  pl.BlockSpec((pl.Element(tm), pl.Element(D)), lambda b, i: (rows(b, i), 0)),
                  pl.BlockSpec((1, D), lambda b, i: (0, 0))],
        out_specs=pl.BlockSpec((None, tm, D), lambda b, i: (b, i, 0)),
        out_shape=jax.ShapeDtypeStruct((B, S, D), F32),
        compiler_params=_cparams(2),
        name="matmul_residual_final",
    )(x, w, res, g.reshape(1, D))


def _fill_padded(pad_ref, x_ref, L):
    rows, width = pad_ref.shape
    pad_ref[0:HALO, :] = jnp.zeros((HALO, width), F32)
    pad_ref[HALO + L:rows, :] = jnp.zeros((rows - HALO - L, width), F32)
    pad_ref[HALO:HALO + L, :] = x_ref[...].astype(F32)


def _chunk_start(k, rows, L):
    return pl.multiple_of(jnp.minimum(k * rows, L - rows), 16)


POOL_CHUNK = 512


def _pool_kernel(u_ref, w_ref, s_ref, o_ref, pad_ref, *, L):
    g = pl.program_id(1)
    _fill_padded(pad_ref, u_ref, L)
    rows = POOL_CHUNK
    n_chunks = pl.cdiv(L, rows)

    def run(window):
        hw = window // 2

        def chunk(k, carry):
            start = _chunk_start(k, rows, L)
            win = pad_ref[pl.ds(start, rows + 2 * HALO), :]
            total = win[HALO - hw:HALO - hw + rows]
            for i in range(1, window):
                total = total + win[HALO - hw + i:HALO - hw + i + rows]
            t = start + lax.broadcasted_iota(jnp.int32, (rows, POOL_GROUP), 0)
            cnt = (jnp.minimum(t + hw, L) - jnp.maximum(t - hw, 0)).astype(F32)
            d = total / cnt - win[HALO:HALO + rows]
            y = jnp.dot(d.astype(BF16), w_ref[...], preferred_element_type=F32)
            o_ref[pl.ds(start, rows), :] = (y * s_ref[...]).astype(o_ref.dtype)
            return carry

        lax.fori_loop(0, n_chunks, chunk, 0)

    for gi, window in enumerate(POOL_WINDOWS):
        pl.when(g == gi)(functools.partial(run, window))


def pool_mixer(proj, pool_w, pool_scale, *, B, L):
    n_groups = len(POOL_WINDOWS)
    width = n_groups * POOL_GROUP
    return pl.pallas_call(
        functools.partial(_pool_kernel, L=L),
        grid=(B, n_groups),
        in_specs=[pl.BlockSpec((None, L, POOL_GROUP), lambda b, g: (b, 0, g)),
                  pl.BlockSpec((None, POOL_GROUP, POOL_GROUP), lambda b, g: (g, 0, 0)),
                  pl.BlockSpec((1, POOL_GROUP), lambda b, g: (0, g))],
        out_specs=pl.BlockSpec((None, L, POOL_GROUP), lambda b, g: (b, 0, g)),
        out_shape=jax.ShapeDtypeStruct((B, L, width), BF16),
        scratch_shapes=[pltpu.VMEM((L + 2 * HALO, POOL_GROUP), F32)],
        compiler_params=_cparams(2),
        name="pool_mixer",
    )(proj, pool_w, pool_scale.reshape(1, width))


def _fft_factor(L):
    n1 = -(-(2 * L - 1) // FFT_N2)
    return -(-n1 // STAGE2_K1) * STAGE2_K1


def _valid_n1(L):
    return -(-(-(-L // FFT_N2)) // 8) * 8


PREP_CHUNK = 512


def _prep_kernel(u0_ref, u1_ref, w_ref, b_ref, o_ref, pad_ref, y0_ref, y1_ref, *, L):
    n2_count, n1, _ = o_ref.shape
    n1v = y0_ref.shape[0] // n2_count
    rows = PREP_CHUNK
    w0, w1, w2 = w_ref[0:1, :], w_ref[1:2, :], w_ref[2:3, :]
    bias = b_ref[...]

    for u_ref, y_ref in ((u0_ref, y0_ref), (u1_ref, y1_ref)):
        _fill_padded(pad_ref, u_ref, L)
        y_ref[L:y_ref.shape[0], :] = jnp.zeros((y_ref.shape[0] - L, LANES), F32)

        def chunk(k, carry, y_ref=y_ref):
            start = _chunk_start(k, rows, L)
            win = pad_ref[pl.ds(start, rows + 2 * HALO), :]
            y_ref[pl.ds(start, rows), :] = (
                win[HALO - 1:HALO - 1 + rows] * w0 + win[HALO:HALO + rows] * w1
                + win[HALO + 1:HALO + 1 + rows] * w2 + bias)
            return carry

        lax.fori_loop(0, pl.cdiv(L, rows), chunk, 0)

    def per_n2(n2, carry):
        rows_n2 = pl.ds(n2, n1v, stride=n2_count)
        o_ref[n2, 0:n1v, :] = _pack(y0_ref[rows_n2, :], y1_ref[rows_n2, :])
        return carry

    lax.fori_loop(0, n2_count, per_n2, 0, unroll=4)
    o_ref[:, n1v:n1, :] = jnp.zeros((n2_count, n1 - n1v, LANES), o_ref.dtype)


def hyena_prep(proj, conv_w, conv_b, *, B, L, N1, col0, width):
    N2 = FFT_N2
    n_cb = width // LANES
    n1v = _valid_n1(L)
    assert col0 % LANES == 0 and n1v <= N1
    cols = lambda s, c: col0 // LANES + s * n_cb + c
    seq = pltpu.VMEM((n1v * N2, LANES), F32)
    return pl.pallas_call(
        functools.partial(_prep_kernel, L=L),
        grid=(3, B // 2, n_cb),
        in_specs=[pl.BlockSpec((None, L, LANES), lambda s, p, c: (2 * p, 0, cols(s, c))),
                  pl.BlockSpec((None, L, LANES), lambda s, p, c: (2 * p + 1, 0, cols(s, c))),
                  pl.BlockSpec((3, LANES), lambda s, p, c: (0, s * n_cb + c)),
                  pl.BlockSpec((1, LANES), lambda s, p, c: (0, s * n_cb + c))],
        out_specs=pl.BlockSpec((None, None, None, N2, N1, LANES),
                               lambda s, p, c: (s, p, c, 0, 0, 0)),
        out_shape=jax.ShapeDtypeStruct((3, B // 2, n_cb, N2, N1, LANES), U32),
        scratch_shapes=[pltpu.VMEM((L + 2 * HALO, LANES), F32), seq, seq],
        compiler_params=_cparams(3),
        name="hyena_prep",
    )(proj, proj, conv_w, conv_b.reshape(1, 3 * width))


def _unpack_kernel(x_ref, o_ref):
    n2_count, n1v, _ = x_ref.shape

    def per_n2(n2, carry):
        lo, hi = _unpack(x_ref[n2])
        o_ref[0, pl.ds(n2, n1v, stride=n2_count), :] = lo
        o_ref[1, pl.ds(n2, n1v, stride=n2_count), :] = hi
        return carry

    lax.fori_loop(0, n2_count, per_n2, 0, unroll=4)


def unpack_pairs(z, *, L):
    P, chunks, N2, N1, _ = z.shape
    n1v = _valid_n1(L)
    out = pl.pallas_call(
        _unpack_kernel,
        grid=(P, chunks),
        in_specs=[pl.BlockSpec((None, None, N2, n1v, LANES), lambda p, c: (p, c, 0, 0, 0))],
        out_specs=pl.BlockSpec((None, 2, None, n1v * N2, LANES), lambda p, c: (p, 0, c, 0, 0)),
        out_shape=jax.ShapeDtypeStruct((P, 2, chunks, n1v * N2, LANES), F32),
        compiler_params=_cparams(2),
        name="unpack_pairs",
    )(z)
    return out.reshape(2 * P, chunks, n1v * N2, LANES)


FILTER_N2 = 4


def _filter_kernel(bands_ref, deltas_ref, w1t_ref, w1c_ref, w1s_ref, b1_ref, w2_ref, b2_ref,
                   w3_ref, b3_ref, fr_ref, w4_ref, skip_ref, o_ref, *, L, N, C):
    n1 = o_ref.shape[3]
    rows = FILTER_N2 * n1
    n2_base = pl.program_id(0) * FILTER_N2
    dot_hi = functools.partial(jnp.dot, precision=HI, preferred_element_type=F32)

    def cyclic_index(shape, axis):
        r = lax.broadcasted_iota(jnp.int32, shape, axis)
        j = sum((r >= k * n1).astype(jnp.int32) for k in range(1, FILTER_N2))
        return (r - j * n1) * FFT_N2 + n2_base + j

    m_row = cyclic_index((1, rows), 1)
    pos_row = jnp.where(m_row < L, m_row, N - m_row).astype(F32)
    t_row = pos_row / (L - 1)
    ang = bands_ref[...] * ((2.0 * math.pi / L) * pos_row)
    fr = fr_ref[...]
    pre = (w1t_ref[...] * t_row + dot_hi(w1c_ref[...], jnp.cos(ang))
           + dot_hi(w1s_ref[...], -jnp.sin(ang)) + b1_ref[...])
    h = jnp.sin(fr * pre)
    h = jnp.sin(fr * (dot_hi(w2_ref[...], h) + b2_ref[...]))
    h = jnp.sin(fr * (dot_hi(w3_ref[...], h) + b3_ref[...]))

    m = cyclic_index((rows, 1), 0)
    fwd = m < L
    bwd = m > N - L
    t = jnp.where(fwd, m, N - m).astype(F32) / (L - 1)
    decay = jnp.exp(-t * deltas_ref[...])
    taps_of = lambda col: lax.dot_general(
        h, w4_ref[:, col * C:(col + 1) * C], (((0,), (0,)), ((), ())),
        preferred_element_type=F32)
    for order in range(2):
        k = jnp.where(fwd, taps_of(2 * order), jnp.where(bwd, taps_of(2 * order + 1), 0.0)) * decay
        k = jnp.where(m == 0, k + skip_ref[order:order + 1, :], k)
        for c, kc in enumerate(_lane_chunks(k)):
            for j in range(FILTER_N2):
                o_ref[order, c, j] = kc[j * n1:(j + 1) * n1]


def hyena_filter_taps(w1, b1, w2, b2, w3, b3, freq, w4, skip, *, L, N1):
    C = skip.shape[1]
    fw = w2.shape[0]
    N2 = FFT_N2
    bands = jnp.linspace(1e-4, FILTER_BANDS - 1, FILTER_BANDS, dtype=F32).reshape(FILTER_BANDS, 1)
    deltas = jnp.abs(jnp.linspace(math.log(DECAY_TARGET) / SLOW_DECAY_PCT,
                                  math.log(DECAY_TARGET) / FAST_DECAY_PCT, C, dtype=F32)).reshape(1, C)
    col = lambda a: a.reshape(fw, 1)
    args = (bands, deltas, w1[0:1].T, w1[1:1 + FILTER_BANDS].T, w1[1 + FILTER_BANDS:].T,
            col(b1), w2.T, col(b2), w3.T, col(b3), col(freq), w4, skip)
    small = lambda a: pl.BlockSpec(a.shape, lambda i: (0,) * a.ndim)
    return pl.pallas_call(
        functools.partial(_filter_kernel, L=L, N=N1 * N2, C=C),
        grid=(N2 // FILTER_N2,),
        in_specs=[small(a) for a in args],
        out_specs=pl.BlockSpec((2, C // LANES, FILTER_N2, N1, LANES), lambda i: (0, 0, i, 0, 0)),
        out_shape=jax.ShapeDtypeStruct((2, C // LANES, N2, N1, LANES), F32),
        compiler_params=_cparams(1),
        name="hyena_filter_taps",
    )(*args)


def _real_form(m):
    return np.block([[m.real, -m.imag], [m.imag, m.real]])


@functools.lru_cache(maxsize=None)
def _dft_tables(N1, N2):
    N = N1 * N2
    k1 = np.arange(N1)
    g1 = np.exp(-2j * np.pi * ((k1[:, None] * k1[None, :]) % N1) / N1)
    stage1 = _real_form(g1)
    stage1_inv = _real_form(np.conj(g1))
    n2 = np.arange(N2)
    k = k1[:, None, None] + N1 * n2[None, :, None]
    mk = np.exp(-2j * np.pi * ((k * n2[None, None, :]) % N) / N)
    stage2 = np.stack([_real_form(mk[i]) for i in range(N1)])
    stage2_inv = np.stack([_real_form(np.conj(mk[i]).T) for i in range(N1)])
    return tuple(a.astype(BF16) for a in (stage1, stage1_inv, stage2, stage2_inv))


STAGE1_J = 8
STAGE1_CHUNKS = 4


STAGE2_K1 = 8


def _time_spec(jb, cb, n1, lead):
    if lead is None:
        return pl.BlockSpec((None, cb, jb, n1, LANES), lambda p, j, c: (p, c, j, 0, 0))
    return pl.BlockSpec((None, None, cb, jb, n1, LANES), lambda p, j, c: (lead, p, c, j, 0, 0))


def _freq_spec_n2(jb, cb, n1):
    kb = STAGE2_K1
    return pl.BlockSpec((None, cb, n1 // kb, jb * kb, LANES), lambda p, j, c: (p, c, 0, j, 0))


def _load_time(x_ref, j):
    return jnp.concatenate([x_ref[c, j] for c in range(x_ref.shape[0])], axis=1)


def _load_freq(x_ref, j):
    kb = STAGE2_K1
    rows = slice(j * kb, (j + 1) * kb)
    return jnp.concatenate(
        [jnp.concatenate([x_ref[c, k, rows, :] for k in range(x_ref.shape[1])], axis=0)
         for c in range(x_ref.shape[0])], axis=1)


def _store_time(o_ref, j, word):
    for c, w in enumerate(_lane_chunks(word)):
        o_ref[c, j] = w


def _store_freq(o_ref, j, word):
    kb = STAGE2_K1
    for c, w in enumerate(_lane_chunks(word)):
        for k in range(o_ref.shape[1]):
            o_ref[c, k, j * kb:(j + 1) * kb, :] = w[k * kb:(k + 1) * kb]


def _stage1_matmul(m_ref, word, real_input):
    n1 = word.shape[0]
    if real_input:
        return jnp.dot(m_ref[:, 0:n1], word.astype(BF16), preferred_element_type=F32)
    re, im = _unpack(word)
    return jnp.dot(m_ref[...], jnp.concatenate([re, im], axis=0).astype(BF16),
                   preferred_element_type=F32)


def _stage1_kernel(m_ref, x_ref, o_ref, *, real_input):
    n1 = x_ref.shape[2]
    for j in range(STAGE1_J):
        y = _stage1_matmul(m_ref, _load_time(x_ref, j), real_input)
        _store_freq(o_ref, j, _pack(y[:n1], y[n1:]))


def dft_stage1(mat, x, lead=None):
    P, chunks, N2, N1, _ = x.shape[-5:]
    jb, cb, kb = STAGE1_J, STAGE1_CHUNKS, STAGE2_K1
    return pl.pallas_call(
        functools.partial(_stage1_kernel, real_input=(x.dtype == F32)),
        grid=(P, N2 // jb, chunks // cb),
        in_specs=[pl.BlockSpec(mat.shape, lambda p, j, c: (0, 0)),
                  _time_spec(jb, cb, N1, lead)],
        out_specs=_freq_spec_n2(jb, cb, N1),
        out_shape=jax.ShapeDtypeStruct((P, chunks, N1 // kb, N2 * kb, LANES), U32),
        compiler_params=_cparams(3),
        name="dft_stage1",
    )(mat, x)


def _stage1_gate_kernel(m_ref, x_ref, gate_ref, o_ref):
    n1 = o_ref.shape[2]
    for j in range(STAGE1_J):
        y = _stage1_matmul(m_ref, _load_freq(x_ref, j), False)
        g_lo, g_hi = _unpack(_load_time(gate_ref, j))
        _store_time(o_ref, j, _pack(y[:n1] * g_lo, y[n1:] * g_hi))


def dft_stage1_inverse_gate(mat, x, gates, lead):
    P, chunks, _, _, _ = x.shape
    N2, N1 = gates.shape[-3], gates.shape[-2]
    jb, cb = STAGE1_J, STAGE1_CHUNKS
    return pl.pallas_call(
        _stage1_gate_kernel,
        grid=(P, N2 // jb, chunks // cb),
        in_specs=[pl.BlockSpec(mat.shape, lambda p, j, c: (0, 0)),
                  _freq_spec_n2(jb, cb, N1),
                  _time_spec(jb, cb, N1, lead)],
        out_specs=_time_spec(jb, cb, N1, None),
        out_shape=jax.ShapeDtypeStruct((P, chunks, N2, N1, LANES), U32),
        compiler_params=_cparams(3),
        name="dft_stage1_inverse_gate",
    )(mat, x, gates)


def _stage1_gate_stage1_kernel(mi_ref, mf_ref, x_ref, gate_ref, o_ref):
    n1 = gate_ref.shape[2]
    for j in range(STAGE1_J):
        y = _stage1_matmul(mi_ref, _load_freq(x_ref, j), False)
        g_lo, g_hi = _unpack(_load_time(gate_ref, j))
        z = _stage1_matmul(mf_ref, _pack(y[:n1] * g_lo, y[n1:] * g_hi), False)
        _store_freq(o_ref, j, _pack(z[:n1], z[n1:]))


def dft_stage1_inverse_gate_forward(mat_inv, mat_fwd, x, gates, lead):
    P, chunks, _, _, _ = x.shape
    N2, N1 = gates.shape[-3], gates.shape[-2]
    jb, cb = STAGE1_J, STAGE1_CHUNKS
    return pl.pallas_call(
        _stage1_gate_stage1_kernel,
        grid=(P, N2 // jb, chunks // cb),
        in_specs=[pl.BlockSpec(mat_inv.shape, lambda p, j, c: (0, 0)),
                  pl.BlockSpec(mat_fwd.shape, lambda p, j, c: (0, 0)),
                  _freq_spec_n2(jb, cb, N1),
                  _time_spec(jb, cb, N1, lead)],
        out_specs=_freq_spec_n2(jb, cb, N1),
        out_shape=jax.ShapeDtypeStruct(x.shape, U32),
        compiler_params=_cparams(3),
        name="dft_stage1_inverse_gate_forward",
    )(mat_inv, mat_fwd, x, gates)


def _freq_spec_k1(chunks, n2):
    kb = STAGE2_K1
    return pl.BlockSpec((None, chunks, None, n2 * kb, LANES), lambda p, k: (p, 0, k, 0, 0))


def _stage2_rows(ref, i):
    return pl.ds(i, ref.shape[1] // STAGE2_K1, stride=STAGE2_K1)


def _stage2_operand(a_ref, i):
    rows = _stage2_rows(a_ref, i)
    word = jnp.concatenate([a_ref[c, rows, :] for c in range(a_ref.shape[0])], axis=1)
    re, im = _unpack(word)
    return jnp.concatenate([re, im], axis=0).astype(BF16)


def _stage2_spectrum_kernel(m_ref, a_ref, o_ref, *, scale):
    n2 = o_ref.shape[1]
    for i in range(STAGE2_K1):
        x = jnp.dot(m_ref[i], _stage2_operand(a_ref, i), preferred_element_type=F32) * scale
        o_ref[i] = _pack(x[:n2], x[n2:])


def filter_spectrum(stage2, a, *, N2):
    kb = STAGE2_K1
    _, chunks, n_kb, _, _ = a.shape
    N1 = n_kb * kb
    C = chunks * LANES
    return pl.pallas_call(
        functools.partial(_stage2_spectrum_kernel, scale=1.0 / (N1 * N2)),
        grid=(2, n_kb),
        in_specs=[pl.BlockSpec((kb, 2 * N2, 2 * N2), lambda o, k: (k, 0, 0)),
                  _freq_spec_k1(chunks, N2)],
        out_specs=pl.BlockSpec((None, kb, N2, C), lambda o, k: (o, k, 0, 0)),
        out_shape=jax.ShapeDtypeStruct((2, N1, N2, C), U32),
        compiler_params=_cparams(2),
        name="filter_spectrum",
    )(stage2, a)


def _stage2_conv_kernel(mf_ref, mi_ref, a_ref, k_ref, o_ref):
    n2 = k_ref.shape[1]
    for i in range(STAGE2_K1):
        x = jnp.dot(mf_ref[i], _stage2_operand(a_ref, i), preferred_element_type=F32)
        xr, xi = x[:n2], x[n2:]
        kr, ki = _unpack(k_ref[i])
        y = jnp.concatenate([xr * kr - xi * ki, xr * ki + xi * kr], axis=0).astype(BF16)
        b = jnp.dot(mi_ref[i], y, preferred_element_type=F32)
        rows = _stage2_rows(o_ref, i)
        for c, word in enumerate(_lane_chunks(_pack(b[:n2], b[n2:]))):
            o_ref[c, rows, :] = word


def dft_stage2_conv(stage2, stage2_inv, a, kspec, order):
    kb = STAGE2_K1
    P, chunks, n_kb, _, _ = a.shape
    N2, C = kspec.shape[2], kspec.shape[3]
    seq = pl.BlockSpec((None, chunks, None, N2 * kb, LANES), lambda k, p: (p, 0, k, 0, 0))
    return pl.pallas_call(
        _stage2_conv_kernel,
        grid=(n_kb, P),
        in_specs=[pl.BlockSpec((kb, 2 * N2, 2 * N2), lambda k, p: (k, 0, 0)),
                  pl.BlockSpec((kb, 2 * N2, 2 * N2), lambda k, p: (k, 0, 0)),
                  seq,
                  pl.BlockSpec((None, kb, N2, C), lambda k, p: (order, k, 0, 0))],
        out_specs=seq,
        out_shape=jax.ShapeDtypeStruct(a.shape, U32),
        compiler_params=_cparams(2),
        name="dft_stage2_conv",
    )(stage2, stage2_inv, a, kspec)


def hyena_mixer(proj, conv_w, conv_b, w1, b1, w2, b2, w3, b3, freq, w4, skip, *, B, L, col0):
    C = skip.shape[1]
    N1, N2 = _fft_factor(L), FFT_N2
    stage1, stage1_inv, stage2, stage2_inv = _dft_tables(N1, N2)

    taps = hyena_filter_taps(w1, b1, w2, b2, w3, b3, freq, w4, skip, L=L, N1=N1)
    kspec = filter_spectrum(stage2, dft_stage1(stage1, taps), N2=N2)

    streams = hyena_prep(proj, conv_w, conv_b, B=B, L=L, N1=N1, col0=col0, width=C)
    a = dft_stage1(stage1, streams, lead=0)
    b = dft_stage2_conv(stage2, stage2_inv, a, kspec, 0)
    a = dft_stage1_inverse_gate_forward(stage1_inv, stage1, b, streams, 1)
    b = dft_stage2_conv(stage2, stage2_inv, a, kspec, 1)
    z = dft_stage1_inverse_gate(stage1_inv, b, streams, 2)
    return unpack_pairs(z, L=L)


def _trunk(x, meta_tokens, norm1_g, in_proj, pool_w, pool_scale, hy_conv_w, hy_conv_b,
           flt_w1, flt_b1, flt_w2, flt_b2, flt_w3, flt_b3, flt_freq, flt_w4, hy_skip,
           out_proj, norm2_g, w_gate, w_up, w_down, final_g):
    B, S, D = x.shape
    L = S + N_META
    T = B * L
    depth = in_proj.shape[0]
    pool_width = pool_scale.shape[1]
    meta = jnp.broadcast_to(meta_tokens[None].astype(x.dtype), (B, N_META, D))
    h = jnp.concatenate([meta, x], axis=1).reshape(T, D)
    xn = norm_rows(h, norm1_g[0])
    for l in range(depth):
        proj = matmul_resident(xn, in_proj, l).reshape(B, L, in_proj.shape[2])
        a = pool_mixer(proj, pool_w[l], pool_scale[l], B=B, L=L)
        b = hyena_mixer(proj, hy_conv_w[l], hy_conv_b[l], flt_w1[l], flt_b1[l], flt_w2[l],
                        flt_b2[l], flt_w3[l], flt_b3[l], flt_freq[l], flt_w4[l], hy_skip[l],
                        B=B, L=L, col0=pool_width)
        h, xn = mixer_residual_norm(a, b, out_proj, l, h, norm2_g[l], B=B, L=L)
        act = swiglu(xn, w_gate, w_up, l, tn=512)
        if l + 1 < depth:
            h, xn = matmul_residual_norm(act, w_down, l, h, norm1_g[l + 1])
    return matmul_residual_final(act, w_down, depth - 1, h, final_g, B=B, L=L, skip=N_META)


def kernel(x_prompt, x_sample, meta_tokens, norm1_g, in_proj, pool_w, pool_scale, hy_conv_w, hy_conv_b, flt_w1, flt_b1, flt_w2, flt_b2, flt_w3, flt_b3, flt_freq, flt_w4, hy_skip, out_proj, norm2_g, w_gate, w_up, w_down, final_g):
    params = (meta_tokens, norm1_g, in_proj.astype(BF16), pool_w.astype(BF16), pool_scale,
              hy_conv_w, hy_conv_b, flt_w1, flt_b1, flt_w2, flt_b2, flt_w3, flt_b3, flt_freq,
              flt_w4, hy_skip, out_proj.astype(BF16), norm2_g, w_gate.astype(BF16),
              w_up.astype(BF16), w_down.astype(BF16), final_g)
    return (_trunk(x_prompt, *params), _trunk(x_sample, *params))
```

```python
import functools
import math

import numpy as np
import jax
import jax.numpy as jnp
from jax import lax
from jax.experimental import pallas as pl
from jax.experimental.pallas import tpu as pltpu

F32 = jnp.float32
BF16 = jnp.bfloat16
U32 = jnp.uint32

EPS = 1e-6
N_META = 16
POOL_WINDOWS = (2, 4, 8, 16)
POOL_GROUP = 256
FILTER_BANDS = 16
DECAY_TARGET = 1e-2
FAST_DECAY_PCT = 0.3
SLOW_DECAY_PCT = 1.5

LANES = 128
HALO = 8
FFT_N2 = 80
ROWS_BIG = 1040
ROWS_SMALL = 272
ROWS_FINAL = 256
VMEM_LIMIT = 56 * 1024 * 1024
HI = lax.Precision.HIGHEST


def _cparams(n_axes):
    return pltpu.CompilerParams(
        dimension_semantics=("arbitrary",) * n_axes,
        vmem_limit_bytes=VMEM_LIMIT)


def _resident_layer(w, layer):
    return pl.BlockSpec((None,) + w.shape[1:], lambda *_: (layer, 0, 0),
                        pipeline_mode=pl.Buffered(1))


def _unpack(word):
    lo = lax.bitcast_convert_type(word << 16, F32)
    hi = lax.bitcast_convert_type(word & jnp.uint32(0xFFFF0000), F32)
    return lo, hi


def _pack(lo, hi):
    lo = lax.bitcast_convert_type(lo.astype(BF16).astype(F32), U32)
    hi = lax.bitcast_convert_type(hi.astype(BF16).astype(F32), U32)
    return hi | (lo >> 16)


def _words_to_bf16(word):
    return pltpu.bitcast(word, BF16)


def _f32_to_words(x):
    return pltpu.bitcast(x.astype(BF16), U32)


def _lane_chunks(x):
    return [x[:, c * LANES:(c + 1) * LANES] for c in range(x.shape[1] // LANES)]


def _rmsnorm_rows(x, g):
    ms = jnp.mean(x * x, axis=-1, keepdims=True)
    return x * lax.rsqrt(ms + EPS) * g


def _norm_kernel(x_ref, g_ref, o_ref):
    o_ref[...] = _rmsnorm_rows(x_ref[...], g_ref[...]).astype(o_ref.dtype)


def norm_rows(x, g):
    T, D = x.shape
    tm = ROWS_BIG
    return pl.pallas_call(
        _norm_kernel,
        grid=(pl.cdiv(T, tm),),
        in_specs=[pl.BlockSpec((tm, D), lambda i: (i, 0)),
                  pl.BlockSpec((1, D), lambda i: (0, 0))],
        out_specs=pl.BlockSpec((tm, D), lambda i: (i, 0)),
        out_shape=jax.ShapeDtypeStruct((T, D), BF16),
        compiler_params=_cparams(1),
        name="norm_rows",
    )(x, g.reshape(1, D))


MATMUL_COLS = 1024


def _matmul_kernel(x_ref, w_ref, o_ref):
    x = x_ref[...]
    for c in range(0, o_ref.shape[1], MATMUL_COLS):
        o_ref[:, c:c + MATMUL_COLS] = jnp.dot(
            x, w_ref[:, c:c + MATMUL_COLS], preferred_element_type=F32).astype(o_ref.dtype)


def matmul_resident(x, w, layer):
    T, K = x.shape
    n_out = w.shape[2]
    tm = ROWS_BIG
    return pl.pallas_call(
        _matmul_kernel,
        grid=(pl.cdiv(T, tm),),
        in_specs=[pl.BlockSpec((tm, K), lambda i: (i, 0)), _resident_layer(w, layer)],
        out_specs=pl.BlockSpec((tm, n_out), lambda i: (i, 0)),
        out_shape=jax.ShapeDtypeStruct((T, n_out), BF16),
        compiler_params=_cparams(1),
        name="matmul_resident",
    )(x, w)


def _swiglu_kernel(x_ref, wg_ref, wu_ref, o_ref):
    x = x_ref[...]
    gate = jnp.dot(x, wg_ref[...], preferred_element_type=F32)
    up = jnp.dot(x, wu_ref[...], preferred_element_type=F32)
    o_ref[...] = (gate * jax.nn.sigmoid(gate) * up).astype(o_ref.dtype)


def swiglu(x, wg, wu, layer, *, tn):
    T, K = x.shape
    n_out = wg.shape[2]
    tm = ROWS_BIG
    return pl.pallas_call(
        _swiglu_kernel,
        grid=(n_out // tn, pl.cdiv(T, tm)),
        in_specs=[pl.BlockSpec((tm, K), lambda j, i: (i, 0)),
                  pl.BlockSpec((None, K, tn), lambda j, i: (layer, 0, j)),
                  pl.BlockSpec((None, K, tn), lambda j, i: (layer, 0, j))],
        out_specs=pl.BlockSpec((tm, tn), lambda j, i: (i, j)),
        out_shape=jax.ShapeDtypeStruct((T, n_out), BF16),
        compiler_params=_cparams(2),
        name="swiglu",
    )(x, wg, wu)


def _mixer_residual_norm_kernel(a_ref, b_ref, w_ref, r_ref, g_ref, h_ref, xn_ref):
    ka = a_ref.shape[1]
    b = jnp.concatenate([b_ref[c] for c in range(b_ref.shape[0])], axis=1).astype(BF16)
    acc = (r_ref[...] + jnp.dot(a_ref[...], w_ref[0:ka, :], preferred_element_type=F32)
           + jnp.dot(b, w_ref[ka:, :], preferred_element_type=F32))
    h_ref[...] = acc
    xn_ref[...] = _rmsnorm_rows(acc, g_ref[...]).astype(xn_ref.dtype)


def mixer_residual_norm(a, b, w, layer, res, g, *, B, L):
    D = res.shape[1]
    ka = a.shape[2]
    chunks = b.shape[1]
    tm = ROWS_SMALL
    h, xn = pl.pallas_call(
        _mixer_residual_norm_kernel,
        grid=(B, pl.cdiv(L, tm)),
        in_specs=[pl.BlockSpec((None, tm, ka), lambda s, i: (s, i, 0)),
                  pl.BlockSpec((None, chunks, tm, LANES), lambda s, i: (s, 0, i, 0)),
                  _resident_layer(w, layer),
                  pl.BlockSpec((None, tm, D), lambda s, i: (s, i, 0)),
                  pl.BlockSpec((1, D), lambda s, i: (0, 0))],
        out_specs=[pl.BlockSpec((None, tm, D), lambda s, i: (s, i, 0)),
                   pl.BlockSpec((None, tm, D), lambda s, i: (s, i, 0))],
        out_shape=[jax.ShapeDtypeStruct((B, L, D), F32), jax.ShapeDtypeStruct((B, L, D), BF16)],
        compiler_params=_cparams(2),
        name="mixer_residual_norm",
    )(a, b, w, res.reshape(B, L, D), g.reshape(1, D))
    return h.reshape(B * L, D), xn.reshape(B * L, D)


def _residual_norm_kernel(x_ref, w_ref, r_ref, g_ref, h_ref, xn_ref):
    acc = r_ref[...] + jnp.dot(x_ref[...], w_ref[...], preferred_element_type=F32)
    h_ref[...] = acc
    xn_ref[...] = _rmsnorm_rows(acc, g_ref[...]).astype(xn_ref.dtype)


def matmul_residual_norm(x, w, layer, res, g):
    T, D = res.shape
    tm = ROWS_SMALL
    return pl.pallas_call(
        _residual_norm_kernel,
        grid=(pl.cdiv(T, tm),),
        in_specs=[pl.BlockSpec((tm, x.shape[1]), lambda i: (i, 0)),
                  _resident_layer(w, layer),
                  pl.BlockSpec((tm, D), lambda i: (i, 0)),
                  pl.BlockSpec((1, D), lambda i: (0, 0))],
        out_specs=[pl.BlockSpec((tm, D), lambda i: (i, 0)),
                   pl.BlockSpec((tm, D), lambda i: (i, 0))],
        out_shape=[jax.ShapeDtypeStruct((T, D), F32), jax.ShapeDtypeStruct((T, D), BF16)],
        compiler_params=_cparams(1),
        name="matmul_residual_norm",
    )(x, w, res, g.reshape(1, D))


def _residual_final_kernel(x_ref, w_ref, r_ref, g_ref, y_ref):
    acc = r_ref[...] + jnp.dot(x_ref[...], w_ref[...], preferred_element_type=F32)
    y_ref[...] = _rmsnorm_rows(acc, g_ref[...])


def matmul_residual_final(x, w, layer, res, g, *, B, L, skip):
    K = x.shape[1]
    D = res.shape[1]
    S = L - skip
    tm = ROWS_FINAL
    assert L % 16 == 0 and skip % 16 == 0 and tm % 16 == 0
    rows = lambda b, i: pl.multiple_of(b * L + skip + i * tm, 16)
    return pl.pallas_call(
        _residual_final_kernel,
        grid=(B, S // tm),
        in_specs=[pl.BlockSpec((pl.Element(tm), pl.Element(K)), lambda b, i: (rows(b, i), 0)),
                  _resident_layer(w, layer),
                  pl.BlockSpec((pl.Element(tm), pl.Element(D)), lambda b, i: (rows(b, i), 0)),
                  pl.BlockSpec((1, D), lambda b, i: (0, 0))],
        out_specs=pl.BlockSpec((None, tm, D), lambda b, i: (b, i, 0)),
        out_shape=jax.ShapeDtypeStruct((B, S, D), F32),
        compiler_params=_cparams(2),
        name="matmul_residual_final",
    )(x, w, res, g.reshape(1, D))


def _fill_padded(pad_ref, x_ref, L):
    rows, width = pad_ref.shape
    pad_ref[0:HALO, :] = jnp.zeros((HALO, width), F32)
    pad_ref[HALO + L:rows, :] = jnp.zeros((rows - HALO - L, width), F32)
    pad_ref[HALO:HALO + L, :] = x_ref[...].astype(F32)


def _chunk_start(k, rows, L):
    return pl.multiple_of(jnp.minimum(k * rows, L - rows), 16)


POOL_CHUNK = 512


def _pool_kernel(u_ref, w_ref, s_ref, o_ref, pad_ref, *, L):
    g = pl.program_id(1)
    _fill_padded(pad_ref, u_ref, L)
    rows = POOL_CHUNK
    n_chunks = pl.cdiv(L, rows)

    def run(window):
        hw = window // 2

        def chunk(k, carry):
            start = _chunk_start(k, rows, L)
            win = pad_ref[pl.ds(start, rows + 2 * HALO), :]
            s, w, off = win[:-1] + win[1:], 2, 1
            while w < window:
                s, w, off = s[:-w] + s[w:], 2 * w, off + w // 2
            total = s[HALO - off:HALO - off + rows]
            t = start + lax.broadcasted_iota(jnp.int32, (rows, POOL_GROUP), 0)
            cnt = (jnp.minimum(t + hw, L) - jnp.maximum(t - hw, 0)).astype(F32)
            d = total / cnt - win[HALO:HALO + rows]
            y = jnp.dot(d.astype(BF16), w_ref[...], preferred_element_type=F32)
            o_ref[pl.ds(start, rows), :] = (y * s_ref[...]).astype(o_ref.dtype)
            return carry

        lax.fori_loop(0, n_chunks, chunk, 0)

    for gi, window in enumerate(POOL_WINDOWS):
        pl.when(g == gi)(functools.partial(run, window))


def pool_mixer(proj, pool_w, pool_scale, *, B, L):
    n_groups = len(POOL_WINDOWS)
    width = n_groups * POOL_GROUP
    return pl.pallas_call(
        functools.partial(_pool_kernel, L=L),
        grid=(B, n_groups),
        in_specs=[pl.BlockSpec((None, L, POOL_GROUP), lambda b, g: (b, 0, g)),
                  pl.BlockSpec((None, POOL_GROUP, POOL_GROUP), lambda b, g: (g, 0, 0)),
                  pl.BlockSpec((1, POOL_GROUP), lambda b, g: (0, g))],
        out_specs=pl.BlockSpec((None, L, POOL_GROUP), lambda b, g: (b, 0, g)),
        out_shape=jax.ShapeDtypeStruct((B, L, width), BF16),
        scratch_shapes=[pltpu.VMEM((L + 2 * HALO, POOL_GROUP), F32)],
        compiler_params=_cparams(2),
        name="pool_mixer",
    )(proj, pool_w, pool_scale.reshape(1, width))


def _fft_factor(L):
    n1 = -(-(2 * L - 1) // FFT_N2)
    return -(-n1 // STAGE2_K1) * STAGE2_K1


def _valid_n1(L):
    return -(-(-(-L // FFT_N2)) // 8) * 8


PREP_CHUNK = 512


def _prep_kernel(u0_ref, u1_ref, w_ref, b_ref, o_ref, pad_ref, y0_ref, y1_ref, *, L):
    n2_count, n1, _ = o_ref.shape
    n1v = y0_ref.shape[0] // n2_count
    rows = PREP_CHUNK
    w0, w1, w2 = w_ref[0:1, :], w_ref[1:2, :], w_ref[2:3, :]
    bias = b_ref[...]

    for u_ref, y_ref in ((u0_ref, y0_ref), (u1_ref, y1_ref)):
        _fill_padded(pad_ref, u_ref, L)
        y_ref[L:y_ref.shape[0], :] = jnp.zeros((y_ref.shape[0] - L, LANES), F32)

        def chunk(k, carry, y_ref=y_ref):
            start = _chunk_start(k, rows, L)
            win = pad_ref[pl.ds(start, rows + 2 * HALO), :]
            y_ref[pl.ds(start, rows), :] = (
                win[HALO - 1:HALO - 1 + rows] * w0 + win[HALO:HALO + rows] * w1
                + win[HALO + 1:HALO + 1 + rows] * w2 + bias)
            return carry

        lax.fori_loop(0, pl.cdiv(L, rows), chunk, 0)

    def per_n2(n2, carry):
        rows_n2 = pl.ds(n2, n1v, stride=n2_count)
        o_ref[n2, 0:n1v, :] = _pack(y0_ref[rows_n2, :], y1_ref[rows_n2, :])
        return carry

    lax.fori_loop(0, n2_count, per_n2, 0, unroll=4)
    o_ref[:, n1v:n1, :] = jnp.zeros((n2_count, n1 - n1v, LANES), o_ref.dtype)


def hyena_prep(proj, conv_w, conv_b, *, B, L, N1, col0, width):
    N2 = FFT_N2
    n_cb = width // LANES
    n1v = _valid_n1(L)
    assert col0 % LANES == 0 and n1v <= N1
    cols = lambda s, c: col0 // LANES + s * n_cb + c
    seq = pltpu.VMEM((n1v * N2, LANES), F32)
    return pl.pallas_call(
        functools.partial(_prep_kernel, L=L),
        grid=(3, B // 2, n_cb),
        in_specs=[pl.BlockSpec((None, L, LANES), lambda s, p, c: (2 * p, 0, cols(s, c))),
                  pl.BlockSpec((None, L, LANES), lambda s, p, c: (2 * p + 1, 0, cols(s, c))),
                  pl.BlockSpec((3, LANES), lambda s, p, c: (0, s * n_cb + c)),
                  pl.BlockSpec((1, LANES), lambda s, p, c: (0, s * n_cb + c))],
        out_specs=pl.BlockSpec((None, None, None, N2, N1, LANES),
                               lambda s, p, c: (s, p, c, 0, 0, 0)),
        out_shape=jax.ShapeDtypeStruct((3, B // 2, n_cb, N2, N1, LANES), U32),
        scratch_shapes=[pltpu.VMEM((L + 2 * HALO, LANES), F32), seq, seq],
        compiler_params=_cparams(3),
        name="hyena_prep",
    )(proj, proj, conv_w, conv_b.reshape(1, 3 * width))


def _unpack_kernel(x_ref, o_ref):
    n2_count, n1v, _ = x_ref.shape

    def per_n2(n2, carry):
        lo, hi = _unpack(x_ref[n2])
        o_ref[0, pl.ds(n2, n1v, stride=n2_count), :] = lo
        o_ref[1, pl.ds(n2, n1v, stride=n2_count), :] = hi
        return carry

    lax.fori_loop(0, n2_count, per_n2, 0, unroll=4)


def unpack_pairs(z, *, L):
    P, chunks, N2, N1, _ = z.shape
    n1v = _valid_n1(L)
    out = pl.pallas_call(
        _unpack_kernel,
        grid=(P, chunks),
        in_specs=[pl.BlockSpec((None, None, N2, n1v, LANES), lambda p, c: (p, c, 0, 0, 0))],
        out_specs=pl.BlockSpec((None, 2, None, n1v * N2, LANES), lambda p, c: (p, 0, c, 0, 0)),
        out_shape=jax.ShapeDtypeStruct((P, 2, chunks, n1v * N2, LANES), F32),
        compiler_params=_cparams(2),
        name="unpack_pairs",
    )(z)
    return out.reshape(2 * P, chunks, n1v * N2, LANES)


FILTER_N2 = 4


def _filter_kernel(bands_ref, deltas_ref, w1t_ref, w1c_ref, w1s_ref, b1_ref, w2_ref, b2_ref,
                   w3_ref, b3_ref, fr_ref, w4_ref, skip_ref, o_ref, *, L, N, C):
    n1 = o_ref.shape[3]
    rows = FILTER_N2 * n1
    n2_base = pl.program_id(0) * FILTER_N2
    dot_hi = functools.partial(jnp.dot, precision=HI, preferred_element_type=F32)

    def cyclic_index(shape, axis):
        r = lax.broadcasted_iota(jnp.int32, shape, axis)
        j = sum((r >= k * n1).astype(jnp.int32) for k in range(1, FILTER_N2))
        return (r - j * n1) * FFT_N2 + n2_base + j

    m_row = cyclic_index((1, rows), 1)
    pos_row = jnp.where(m_row < L, m_row, N - m_row).astype(F32)
    t_row = pos_row / (L - 1)
    ang = bands_ref[...] * ((2.0 * math.pi / L) * pos_row)
    fr = fr_ref[...]
    pre = (w1t_ref[...] * t_row + dot_hi(w1c_ref[...], jnp.cos(ang))
           + dot_hi(w1s_ref[...], -jnp.sin(ang)) + b1_ref[...])
    h = jnp.sin(fr * pre)
    h = jnp.sin(fr * (dot_hi(w2_ref[...], h) + b2_ref[...]))
    h = jnp.sin(fr * (dot_hi(w3_ref[...], h) + b3_ref[...]))

    m = cyclic_index((rows, 1), 0)
    fwd = m < L
    bwd = m > N - L
    t = jnp.where(fwd, m, N - m).astype(F32) / (L - 1)
    decay = jnp.exp(-t * deltas_ref[...])
    taps_of = lambda col: lax.dot_general(
        h, w4_ref[:, col * C:(col + 1) * C], (((0,), (0,)), ((), ())),
        preferred_element_type=F32)
    for order in range(2):
        k = jnp.where(fwd, taps_of(2 * order), jnp.where(bwd, taps_of(2 * order + 1), 0.0)) * decay
        k = jnp.where(m == 0, k + skip_ref[order:order + 1, :], k)
        for c, kc in enumerate(_lane_chunks(k)):
            for j in range(FILTER_N2):
                o_ref[order, c, j] = kc[j * n1:(j + 1) * n1]


def hyena_filter_taps(w1, b1, w2, b2, w3, b3, freq, w4, skip, *, L, N1):
    C = skip.shape[1]
    fw = w2.shape[0]
    N2 = FFT_N2
    bands = jnp.linspace(1e-4, FILTER_BANDS - 1, FILTER_BANDS, dtype=F32).reshape(FILTER_BANDS, 1)
    deltas = jnp.abs(jnp.linspace(math.log(DECAY_TARGET) / SLOW_DECAY_PCT,
                                  math.log(DECAY_TARGET) / FAST_DECAY_PCT, C, dtype=F32)).reshape(1, C)
    col = lambda a: a.reshape(fw, 1)
    args = (bands, deltas, w1[0:1].T, w1[1:1 + FILTER_BANDS].T, w1[1 + FILTER_BANDS:].T,
            col(b1), w2.T, col(b2), w3.T, col(b3), col(freq), w4, skip)
    small = lambda a: pl.BlockSpec(a.shape, lambda i: (0,) * a.ndim)
    return pl.pallas_call(
        functools.partial(_filter_kernel, L=L, N=N1 * N2, C=C),
        grid=(N2 // FILTER_N2,),
        in_specs=[small(a) for a in args],
        out_specs=pl.BlockSpec((2, C // LANES, FILTER_N2, N1, LANES), lambda i: (0, 0, i, 0, 0)),
        out_shape=jax.ShapeDtypeStruct((2, C // LANES, N2, N1, LANES), F32),
        compiler_params=_cparams(1),
        name="hyena_filter_taps",
    )(*args)


def _real_form(m):
    return np.block([[m.real, -m.imag], [m.imag, m.real]])


def _interleave(n):
    return np.arange(2 * n).reshape(2, n).T.reshape(-1)


@functools.lru_cache(maxsize=None)
def _dft_tables(N1, N2):
    N = N1 * N2
    k1 = np.arange(N1)
    g1 = np.exp(-2j * np.pi * ((k1[:, None] * k1[None, :]) % N1) / N1)
    il1, il2 = _interleave(N1), _interleave(N2)
    stage1 = _real_form(g1)[il1][:, il1]
    stage1_real = _real_form(g1)[il1][:, :N1]
    stage1_inv = _real_form(np.conj(g1))[il1][:, il1]
    n2 = np.arange(N2)
    k = k1[:, None, None] + N1 * n2[None, :, None]
    mk = np.exp(-2j * np.pi * ((k * n2[None, None, :]) % N) / N)
    fwd = np.stack([_real_form(mk[i]) for i in range(N1)])
    stage2 = fwd[:, :, il2]
    stage2_il = fwd[:, il2][:, :, il2]
    stage2_inv = np.stack([_real_form(np.conj(mk[i]).T) for i in range(N1)])[:, il2]
    return tuple(a.astype(BF16) for a in
                 (stage1, stage1_real, stage1_inv, stage2, stage2_il, stage2_inv))


STAGE1_J = 16
STAGE1_CHUNKS = 4


STAGE2_K1 = 8


def _time_spec(jb, cb, n1, lead):
    if lead is None:
        return pl.BlockSpec((None, cb, jb, n1, LANES), lambda p, j, c: (p, c, j, 0, 0))
    return pl.BlockSpec((None, None, cb, jb, n1, LANES), lambda p, j, c: (lead, p, c, j, 0, 0))


def _freq_spec_n2(jb, cb, n1):
    kb = STAGE2_K1
    return pl.BlockSpec((None, cb, n1 // kb, jb * kb, LANES), lambda p, j, c: (p, c, 0, j, 0))


def _load_time(x_ref, j):
    return jnp.concatenate([x_ref[c, j] for c in range(x_ref.shape[0])], axis=1)


def _load_freq(x_ref, j):
    kb = STAGE2_K1
    rows = slice(j * kb, (j + 1) * kb)
    return jnp.concatenate(
        [jnp.concatenate([x_ref[c, k, rows, :] for k in range(x_ref.shape[1])], axis=0)
         for c in range(x_ref.shape[0])], axis=1)


def _store_time(o_ref, j, word):
    for c, w in enumerate(_lane_chunks(word)):
        o_ref[c, j] = w


def _store_freq(o_ref, j, word):
    kb = STAGE2_K1
    for c, w in enumerate(_lane_chunks(word)):
        for k in range(o_ref.shape[1]):
            o_ref[c, k, j * kb:(j + 1) * kb, :] = w[k * kb:(k + 1) * kb]


def _stage1_matmul(m_ref, word, real_input):
    operand = word.astype(BF16) if real_input else _words_to_bf16(word)
    return jnp.dot(m_ref[...], operand, preferred_element_type=F32)


def _stage1_kernel(m_ref, x_ref, o_ref, *, real_input):
    for j in range(STAGE1_J):
        y = _stage1_matmul(m_ref, _load_time(x_ref, j), real_input)
        _store_freq(o_ref, j, _f32_to_words(y))


def dft_stage1(mat, x, lead=None):
    P, chunks, N2, N1, _ = x.shape[-5:]
    jb, cb, kb = STAGE1_J, STAGE1_CHUNKS, STAGE2_K1
    return pl.pallas_call(
        functools.partial(_stage1_kernel, real_input=(x.dtype == F32)),
        grid=(P, N2 // jb, chunks // cb),
        in_specs=[pl.BlockSpec(mat.shape, lambda p, j, c: (0, 0)),
                  _time_spec(jb, cb, N1, lead)],
        out_specs=_freq_spec_n2(jb, cb, N1),
        out_shape=jax.ShapeDtypeStruct((P, chunks, N1 // kb, N2 * kb, LANES), U32),
        compiler_params=_cparams(3),
        name="dft_stage1",
    )(mat, x)


def _stage1_gate_kernel(m_ref, x_ref, gate_ref, o_ref):
    for j in range(STAGE1_J):
        y = _stage1_matmul(m_ref, _load_freq(x_ref, j), False)
        gate = _words_to_bf16(_load_time(gate_ref, j)).astype(F32)
        _store_time(o_ref, j, _f32_to_words(y * gate))


def dft_stage1_inverse_gate(mat, x, gates, lead):
    P, chunks, _, _, _ = x.shape
    N2, N1 = gates.shape[-3], gates.shape[-2]
    jb, cb = STAGE1_J, STAGE1_CHUNKS
    return pl.pallas_call(
        _stage1_gate_kernel,
        grid=(P, N2 // jb, chunks // cb),
        in_specs=[pl.BlockSpec(mat.shape, lambda p, j, c: (0, 0)),
                  _freq_spec_n2(jb, cb, N1),
                  _time_spec(jb, cb, N1, lead)],
        out_specs=_time_spec(jb, cb, N1, None),
        out_shape=jax.ShapeDtypeStruct((P, chunks, N2, N1, LANES), U32),
        compiler_params=_cparams(3),
        name="dft_stage1_inverse_gate",
    )(mat, x, gates)


def _stage1_gate_stage1_kernel(mi_ref, mf_ref, x_ref, gate_ref, o_ref, z_ref):
    for j in range(STAGE1_J):
        y = _stage1_matmul(mi_ref, _load_freq(x_ref, j), False)
        gate = _words_to_bf16(_load_time(gate_ref, j)).astype(F32)
        z_ref[j] = (y * gate).astype(BF16)
    for j in range(STAGE1_J):
        a = jnp.dot(mf_ref[...], z_ref[j], preferred_element_type=F32)
        _store_freq(o_ref, j, _f32_to_words(a))


def dft_stage1_inverse_gate_forward(mat_inv, mat_fwd, x, gates, lead):
    P, chunks, _, _, _ = x.shape
    N2, N1 = gates.shape[-3], gates.shape[-2]
    jb, cb = STAGE1_J, STAGE1_CHUNKS
    return pl.pallas_call(
        _stage1_gate_stage1_kernel,
        grid=(P, N2 // jb, chunks // cb),
        in_specs=[pl.BlockSpec(mat_inv.shape, lambda p, j, c: (0, 0)),
                  pl.BlockSpec(mat_fwd.shape, lambda p, j, c: (0, 0)),
                  _freq_spec_n2(jb, cb, N1),
                  _time_spec(jb, cb, N1, lead)],
        out_specs=_freq_spec_n2(jb, cb, N1),
        out_shape=jax.ShapeDtypeStruct(x.shape, U32),
        scratch_shapes=[pltpu.VMEM((jb, 2 * N1, cb * LANES), BF16)],
        compiler_params=_cparams(3),
        name="dft_stage1_inverse_gate_forward",
    )(mat_inv, mat_fwd, x, gates)


def _freq_spec_k1(chunks, n2):
    kb = STAGE2_K1
    return pl.BlockSpec((None, chunks, None, n2 * kb, LANES), lambda p, k: (p, 0, k, 0, 0))


def _stage2_rows(ref, i):
    return pl.ds(i, ref.shape[1] // STAGE2_K1, stride=STAGE2_K1)


def _stage2_operand(a_ref, i):
    rows = _stage2_rows(a_ref, i)
    word = jnp.concatenate([a_ref[c, rows, :] for c in range(a_ref.shape[0])], axis=1)
    return _words_to_bf16(word)


def _stage2_spectrum_kernel(m_ref, a_ref, o_ref, *, scale):
    for i in range(STAGE2_K1):
        x = jnp.dot(m_ref[i], _stage2_operand(a_ref, i), preferred_element_type=F32) * scale
        o_ref[i] = _f32_to_words(x)


def filter_spectrum(stage2, a, *, N2):
    kb = STAGE2_K1
    _, chunks, n_kb, _, _ = a.shape
    N1 = n_kb * kb
    C = chunks * LANES
    return pl.pallas_call(
        functools.partial(_stage2_spectrum_kernel, scale=1.0 / (N1 * N2)),
        grid=(2, n_kb),
        in_specs=[pl.BlockSpec((kb, 2 * N2, 2 * N2), lambda o, k: (k, 0, 0)),
                  _freq_spec_k1(chunks, N2)],
        out_specs=pl.BlockSpec((None, kb, N2, C), lambda o, k: (o, k, 0, 0)),
        out_shape=jax.ShapeDtypeStruct((2, N1, N2, C), U32),
        compiler_params=_cparams(2),
        name="filter_spectrum",
    )(stage2, a)


def _stage2_conv_kernel(mf_ref, mi_ref, a_ref, k_ref, o_ref, y_ref):
    n2 = k_ref.shape[1]
    for i in range(STAGE2_K1):
        x = jnp.dot(mf_ref[i], _stage2_operand(a_ref, i), preferred_element_type=F32)
        xr, xi = x[:n2], x[n2:]
        kr, ki = _unpack(k_ref[i])
        y_ref[i] = jnp.concatenate([xr * kr - xi * ki, xr * ki + xi * kr], axis=0).astype(BF16)
    for i in range(STAGE2_K1):
        b = jnp.dot(mi_ref[i], y_ref[i], preferred_element_type=F32)
        rows = _stage2_rows(o_ref, i)
        for c, word in enumerate(_lane_chunks(_f32_to_words(b))):
            o_ref[c, rows, :] = word


def dft_stage2_conv(stage2, stage2_inv, a, kspec, order):
    kb = STAGE2_K1
    P, chunks, n_kb, _, _ = a.shape
    N2, C = kspec.shape[2], kspec.shape[3]
    seq = pl.BlockSpec((None, chunks, None, N2 * kb, LANES), lambda k, p: (p, 0, k, 0, 0))
    return pl.pallas_call(
        _stage2_conv_kernel,
        grid=(n_kb, P),
        in_specs=[pl.BlockSpec((kb, 2 * N2, 2 * N2), lambda k, p: (k, 0, 0)),
                  pl.BlockSpec((kb, 2 * N2, 2 * N2), lambda k, p: (k, 0, 0)),
                  seq,
                  pl.BlockSpec((None, kb, N2, C), lambda k, p: (order, k, 0, 0))],
        out_specs=seq,
        out_shape=jax.ShapeDtypeStruct(a.shape, U32),
        scratch_shapes=[pltpu.VMEM((kb, 2 * N2, C), BF16)],
        compiler_params=_cparams(2),
        name="dft_stage2_conv",
    )(stage2, stage2_inv, a, kspec)


def hyena_mixer(proj, conv_w, conv_b, w1, b1, w2, b2, w3, b3, freq, w4, skip, *, B, L, col0):
    C = skip.shape[1]
    N1, N2 = _fft_factor(L), FFT_N2
    stage1, stage1_real, stage1_inv, stage2, stage2_il, stage2_inv = _dft_tables(N1, N2)

    taps = hyena_filter_taps(w1, b1, w2, b2, w3, b3, freq, w4, skip, L=L, N1=N1)
    kspec = filter_spectrum(stage2_il, dft_stage1(stage1_real, taps), N2=N2)

    streams = hyena_prep(proj, conv_w, conv_b, B=B, L=L, N1=N1, col0=col0, width=C)
    a = dft_stage1(stage1, streams, lead=0)
    b = dft_stage2_conv(stage2, stage2_inv, a, kspec, 0)
    a = dft_stage1_inverse_gate_forward(stage1_inv, stage1, b, streams, 1)
    b = dft_stage2_conv(stage2, stage2_inv, a, kspec, 1)
    z = dft_stage1_inverse_gate(stage1_inv, b, streams, 2)
    return unpack_pairs(z, L=L)


def _trunk(x, meta_tokens, norm1_g, in_proj, pool_w, pool_scale, hy_conv_w, hy_conv_b,
           flt_w1, flt_b1, flt_w2, flt_b2, flt_w3, flt_b3, flt_freq, flt_w4, hy_skip,
           out_proj, norm2_g, w_gate, w_up, w_down, final_g):
    B, S, D = x.shape
    L = S + N_META
    T = B * L
    depth = in_proj.shape[0]
    pool_width = pool_scale.shape[1]
    meta = jnp.broadcast_to(meta_tokens[None].astype(x.dtype), (B, N_META, D))
    h = jnp.concatenate([meta, x], axis=1).reshape(T, D)
    xn = norm_rows(h, norm1_g[0])
    for l in range(depth):
        proj = matmul_resident(xn, in_proj, l).reshape(B, L, in_proj.shape[2])
        a = pool_mixer(proj, pool_w[l], pool_scale[l], B=B, L=L)
        b = hyena_mixer(proj, hy_conv_w[l], hy_conv_b[l], flt_w1[l], flt_b1[l], flt_w2[l],
                        flt_b2[l], flt_w3[l], flt_b3[l], flt_freq[l], flt_w4[l], hy_skip[l],
                        B=B, L=L, col0=pool_width)
        h, xn = mixer_residual_norm(a, b, out_proj, l, h, norm2_g[l], B=B, L=L)
        act = swiglu(xn, w_gate, w_up, l, tn=512)
        if l + 1 < depth:
            h, xn = matmul_residual_norm(act, w_down, l, h, norm1_g[l + 1])
    return matmul_residual_final(act, w_down, depth - 1, h, final_g, B=B, L=L, skip=N_META)


def kernel(x_prompt, x_sample, meta_tokens, norm1_g, in_proj, pool_w, pool_scale, hy_conv_w, hy_conv_b, flt_w1, flt_b1, flt_w2, flt_b2, flt_w3, flt_b3, flt_freq, flt_w4, hy_skip, out_proj, norm2_g, w_gate, w_up, w_down, final_g):
    params = (meta_tokens, norm1_g, in_proj.astype(BF16), pool_w.astype(BF16), pool_scale,
              hy_conv_w, hy_conv_b, flt_w1, flt_b1, flt_w2, flt_b2, flt_w3, flt_b3, flt_freq,
              flt_w4, hy_skip, out_proj.astype(BF16), norm2_g, w_gate.astype(BF16),
              w_up.astype(BF16), w_down.astype(BF16), final_g)
    return (_trunk(x_prompt, *params), _trunk(x_sample, *params))
```

```python
import functools
import math

import numpy as np
import jax
import jax.numpy as jnp
from jax import lax
from jax.experimental import pallas as pl
from jax.experimental.pallas import tpu as pltpu

F32 = jnp.float32
BF16 = jnp.bfloat16
U32 = jnp.uint32

EPS = 1e-6
N_META = 16
POOL_WINDOWS = (2, 4, 8, 16)
POOL_GROUP = 256
FILTER_BANDS = 16
DECAY_TARGET = 1e-2
FAST_DECAY_PCT = 0.3
SLOW_DECAY_PCT = 1.5

LANES = 128
HALO = 8
FFT_N2 = 80
ROWS_BIG = 1040
ROWS_SMALL = 272
ROWS_FINAL = 256
VMEM_LIMIT = 56 * 1024 * 1024
HI = lax.Precision.HIGHEST


def _cparams(n_axes):
    return pltpu.CompilerParams(
        dimension_semantics=("arbitrary",) * n_axes,
        vmem_limit_bytes=VMEM_LIMIT)


def _resident_layer(w, layer):
    return pl.BlockSpec((None,) + w.shape[1:], lambda *_: (layer, 0, 0),
                        pipeline_mode=pl.Buffered(1))


def _unpack(word):
    lo = lax.bitcast_convert_type(word << 16, F32)
    hi = lax.bitcast_convert_type(word & jnp.uint32(0xFFFF0000), F32)
    return lo, hi


def _pack(lo, hi):
    lo = lax.bitcast_convert_type(lo.astype(BF16).astype(F32), U32)
    hi = lax.bitcast_convert_type(hi.astype(BF16).astype(F32), U32)
    return hi | (lo >> 16)


def _words_to_bf16(word):
    return pltpu.bitcast(word, BF16)


def _f32_to_words(x):
    return pltpu.bitcast(x.astype(BF16), U32)


def _lane_chunks(x):
    return [x[:, c * LANES:(c + 1) * LANES] for c in range(x.shape[1] // LANES)]


def _rmsnorm_rows(x, g):
    ms = jnp.mean(x * x, axis=-1, keepdims=True)
    return x * lax.rsqrt(ms + EPS) * g


def _norm_kernel(x_ref, g_ref, o_ref):
    o_ref[...] = _rmsnorm_rows(x_ref[...], g_ref[...]).astype(o_ref.dtype)


def norm_rows(x, g):
    T, D = x.shape
    tm = ROWS_BIG
    return pl.pallas_call(
        _norm_kernel,
        grid=(pl.cdiv(T, tm),),
        in_specs=[pl.BlockSpec((tm, D), lambda i: (i, 0)),
                  pl.BlockSpec((1, D), lambda i: (0, 0))],
        out_specs=pl.BlockSpec((tm, D), lambda i: (i, 0)),
        out_shape=jax.ShapeDtypeStruct((T, D), BF16),
        compiler_params=_cparams(1),
        name="norm_rows",
    )(x, g.reshape(1, D))


MATMUL_COLS = 1024


def _matmul_kernel(x_ref, w_ref, o_ref):
    x = x_ref[...]
    for c in range(0, o_ref.shape[1], MATMUL_COLS):
        o_ref[:, c:c + MATMUL_COLS] = jnp.dot(
            x, w_ref[:, c:c + MATMUL_COLS], preferred_element_type=F32).astype(o_ref.dtype)


def matmul_resident(x, w, layer):
    T, K = x.shape
    n_out = w.shape[2]
    tm = ROWS_BIG
    return pl.pallas_call(
        _matmul_kernel,
        grid=(pl.cdiv(T, tm),),
        in_specs=[pl.BlockSpec((tm, K), lambda i: (i, 0)), _resident_layer(w, layer)],
        out_specs=pl.BlockSpec((tm, n_out), lambda i: (i, 0)),
        out_shape=jax.ShapeDtypeStruct((T, n_out), BF16),
        compiler_params=_cparams(1),
        name="matmul_resident",
    )(x, w)


def _swiglu_kernel(x_ref, wg_ref, wu_ref, o_ref):
    x = x_ref[...]
    gate = jnp.dot(x, wg_ref[...], preferred_element_type=F32)
    up = jnp.dot(x, wu_ref[...], preferred_element_type=F32)
    o_ref[...] = (gate * jax.nn.sigmoid(gate) * up).astype(o_ref.dtype)


def swiglu(x, wg, wu, layer, *, tn):
    T, K = x.shape
    n_out = wg.shape[2]
    tm = ROWS_BIG
    return pl.pallas_call(
        _swiglu_kernel,
        grid=(n_out // tn, pl.cdiv(T, tm)),
        in_specs=[pl.BlockSpec((tm, K), lambda j, i: (i, 0)),
                  pl.BlockSpec((None, K, tn), lambda j, i: (layer, 0, j)),
                  pl.BlockSpec((None, K, tn), lambda j, i: (layer, 0, j))],
        out_specs=pl.BlockSpec((tm, tn), lambda j, i: (i, j)),
        out_shape=jax.ShapeDtypeStruct((T, n_out), BF16),
        compiler_params=_cparams(2),
        name="swiglu",
    )(x, wg, wu)


def _mixer_residual_norm_kernel(a_ref, b_ref, w_ref, r_ref, g_ref, h_ref, xn_ref):
    ka = a_ref.shape[1]
    b = jnp.concatenate([b_ref[c] for c in range(b_ref.shape[0])], axis=1).astype(BF16)
    acc = (r_ref[...] + jnp.dot(a_ref[...], w_ref[0:ka, :], preferred_element_type=F32)
           + jnp.dot(b, w_ref[ka:, :], preferred_element_type=F32))
    h_ref[...] = acc
    xn_ref[...] = _rmsnorm_rows(acc, g_ref[...]).astype(xn_ref.dtype)


def mixer_residual_norm(a, b, w, layer, res, g, *, B, L):
    D = res.shape[1]
    ka = a.shape[2]
    chunks = b.shape[1]
    tm = ROWS_SMALL
    h, xn = pl.pallas_call(
        _mixer_residual_norm_kernel,
        grid=(B, pl.cdiv(L, tm)),
        in_specs=[pl.BlockSpec((None, tm, ka), lambda s, i: (s, i, 0)),
                  pl.BlockSpec((None, chunks, tm, LANES), lambda s, i: (s, 0, i, 0)),
                  _resident_layer(w, layer),
                  pl.BlockSpec((None, tm, D), lambda s, i: (s, i, 0)),
                  pl.BlockSpec((1, D), lambda s, i: (0, 0))],
        out_specs=[pl.BlockSpec((None, tm, D), lambda s, i: (s, i, 0)),
                   pl.BlockSpec((None, tm, D), lambda s, i: (s, i, 0))],
        out_shape=[jax.ShapeDtypeStruct((B, L, D), F32), jax.ShapeDtypeStruct((B, L, D), BF16)],
        compiler_params=_cparams(2),
        name="mixer_residual_norm",
    )(a, b, w, res.reshape(B, L, D), g.reshape(1, D))
    return h.reshape(B * L, D), xn.reshape(B * L, D)


def _residual_norm_kernel(x_ref, w_ref, r_ref, g_ref, h_ref, xn_ref):
    acc = r_ref[...] + jnp.dot(x_ref[...], w_ref[...], preferred_element_type=F32)
    h_ref[...] = acc
    xn_ref[...] = _rmsnorm_rows(acc, g_ref[...]).astype(xn_ref.dtype)


def matmul_residual_norm(x, w, layer, res, g):
    T, D = res.shape
    tm = ROWS_SMALL
    return pl.pallas_call(
        _residual_norm_kernel,
        grid=(pl.cdiv(T, tm),),
        in_specs=[pl.BlockSpec((tm, x.shape[1]), lambda i: (i, 0)),
                  _resident_layer(w, layer),
                  pl.BlockSpec((tm, D), lambda i: (i, 0)),
                  pl.BlockSpec((1, D), lambda i: (0, 0))],
        out_specs=[pl.BlockSpec((tm, D), lambda i: (i, 0)),
                   pl.BlockSpec((tm, D), lambda i: (i, 0))],
        out_shape=[jax.ShapeDtypeStruct((T, D), F32), jax.ShapeDtypeStruct((T, D), BF16)],
        compiler_params=_cparams(1),
        name="matmul_residual_norm",
    )(x, w, res, g.reshape(1, D))


def _residual_final_kernel(x_ref, w_ref, r_ref, g_ref, y_ref):
    acc = r_ref[...] + jnp.dot(x_ref[...], w_ref[...], preferred_element_type=F32)
    y_ref[...] = _rmsnorm_rows(acc, g_ref[...])


def matmul_residual_final(x, w, layer, res, g, *, B, L, skip):
    K = x.shape[1]
    D = res.shape[1]
    S = L - skip
    tm = ROWS_FINAL
    assert L % 16 == 0 and skip % 16 == 0 and tm % 16 == 0
    rows = lambda b, i: pl.multiple_of(b * L + skip + i * tm, 16)
    return pl.pallas_call(
        _residual_final_kernel,
        grid=(B, S // tm),
        in_specs=[pl.BlockSpec((pl.Element(tm), pl.Element(K)), lambda b, i: (rows(b, i), 0)),
                  _resident_layer(w, layer),
                  pl.BlockSpec((pl.Element(tm), pl.Element(D)), lambda b, i: (rows(b, i), 0)),
                  pl.BlockSpec((1, D), lambda b, i: (0, 0))],
        out_specs=pl.BlockSpec((None, tm, D), lambda b, i: (b, i, 0)),
        out_shape=jax.ShapeDtypeStruct((B, S, D), F32),
        compiler_params=_cparams(2),
        name="matmul_residual_final",
    )(x, w, res, g.reshape(1, D))


def _fill_padded(pad_ref, x_ref, L):
    rows, width = pad_ref.shape
    pad_ref[0:HALO, :] = jnp.zeros((HALO, width), F32)
    pad_ref[HALO + L:rows, :] = jnp.zeros((rows - HALO - L, width), F32)
    pad_ref[HALO:HALO + L, :] = x_ref[...].astype(F32)


def _chunk_start(k, rows, L):
    return pl.multiple_of(jnp.minimum(k * rows, L - rows), 16)


POOL_CHUNK = 512


def _pool_kernel(u_ref, w_ref, s_ref, o_ref, pad_ref, *, L):
    g = pl.program_id(1)
    _fill_padded(pad_ref, u_ref, L)
    rows = POOL_CHUNK
    n_chunks = pl.cdiv(L, rows)

    def run(window):
        hw = window // 2

        def chunk(k, carry):
            start = _chunk_start(k, rows, L)
            win = pad_ref[pl.ds(start, rows + 2 * HALO), :]
            s, w, off = win[:-1] + win[1:], 2, 1
            while w < window:
                s, w, off = s[:-w] + s[w:], 2 * w, off + w // 2
            total = s[HALO - off:HALO - off + rows]
            t = start + lax.broadcasted_iota(jnp.int32, (rows, POOL_GROUP), 0)
            cnt = (jnp.minimum(t + hw, L) - jnp.maximum(t - hw, 0)).astype(F32)
            d = total / cnt - win[HALO:HALO + rows]
            y = jnp.dot(d.astype(BF16), w_ref[...], preferred_element_type=F32)
            o_ref[pl.ds(start, rows), :] = (y * s_ref[...]).astype(o_ref.dtype)
            return carry

        lax.fori_loop(0, n_chunks, chunk, 0)

    for gi, window in enumerate(POOL_WINDOWS):
        pl.when(g == gi)(functools.partial(run, window))


def pool_mixer(proj, pool_w, pool_scale, *, B, L):
    n_groups = len(POOL_WINDOWS)
    width = n_groups * POOL_GROUP
    return pl.pallas_call(
        functools.partial(_pool_kernel, L=L),
        grid=(B, n_groups),
        in_specs=[pl.BlockSpec((None, L, POOL_GROUP), lambda b, g: (b, 0, g)),
                  pl.BlockSpec((None, POOL_GROUP, POOL_GROUP), lambda b, g: (g, 0, 0)),
                  pl.BlockSpec((1, POOL_GROUP), lambda b, g: (0, g))],
        out_specs=pl.BlockSpec((None, L, POOL_GROUP), lambda b, g: (b, 0, g)),
        out_shape=jax.ShapeDtypeStruct((B, L, width), BF16),
        scratch_shapes=[pltpu.VMEM((L + 2 * HALO, POOL_GROUP), F32)],
        compiler_params=_cparams(2),
        name="pool_mixer",
    )(proj, pool_w, pool_scale.reshape(1, width))


def _fft_factor(L):
    n1 = -(-(2 * L - 1) // FFT_N2)
    return -(-n1 // STAGE2_K1) * STAGE2_K1


def _valid_n1(L):
    return -(-(-(-L // FFT_N2)) // 8) * 8


PREP_CHUNK = 512


def _prep_kernel(u0_ref, u1_ref, w_ref, b_ref, o_ref, pad_ref, y0_ref, y1_ref, *, L):
    n2_count, n1v, _ = o_ref.shape
    rows = PREP_CHUNK
    w0, w1, w2 = w_ref[0:1, :], w_ref[1:2, :], w_ref[2:3, :]
    bias = b_ref[...]

    for u_ref, y_ref in ((u0_ref, y0_ref), (u1_ref, y1_ref)):
        _fill_padded(pad_ref, u_ref, L)
        y_ref[L:y_ref.shape[0], :] = jnp.zeros((y_ref.shape[0] - L, LANES), F32)

        def chunk(k, carry, y_ref=y_ref):
            start = _chunk_start(k, rows, L)
            win = pad_ref[pl.ds(start, rows + 2 * HALO), :]
            y_ref[pl.ds(start, rows), :] = (
                win[HALO - 1:HALO - 1 + rows] * w0 + win[HALO:HALO + rows] * w1
                + win[HALO + 1:HALO + 1 + rows] * w2 + bias)
            return carry

        lax.fori_loop(0, pl.cdiv(L, rows), chunk, 0)

    def per_n2(n2, carry):
        rows_n2 = pl.ds(n2, n1v, stride=n2_count)
        o_ref[n2] = _pack(y0_ref[rows_n2, :], y1_ref[rows_n2, :])
        return carry

    lax.fori_loop(0, n2_count, per_n2, 0, unroll=4)


def hyena_prep(proj, conv_w, conv_b, *, B, L, col0, width):
    N2 = FFT_N2
    n_cb = width // LANES
    n1v = _valid_n1(L)
    assert col0 % LANES == 0
    cols = lambda s, c: col0 // LANES + s * n_cb + c
    seq = pltpu.VMEM((n1v * N2, LANES), F32)
    return pl.pallas_call(
        functools.partial(_prep_kernel, L=L),
        grid=(3, B // 2, n_cb),
        in_specs=[pl.BlockSpec((None, L, LANES), lambda s, p, c: (2 * p, 0, cols(s, c))),
                  pl.BlockSpec((None, L, LANES), lambda s, p, c: (2 * p + 1, 0, cols(s, c))),
                  pl.BlockSpec((3, LANES), lambda s, p, c: (0, s * n_cb + c)),
                  pl.BlockSpec((1, LANES), lambda s, p, c: (0, s * n_cb + c))],
        out_specs=pl.BlockSpec((None, None, None, N2, n1v, LANES),
                               lambda s, p, c: (s, p, c, 0, 0, 0)),
        out_shape=jax.ShapeDtypeStruct((3, B // 2, n_cb, N2, n1v, LANES), U32),
        scratch_shapes=[pltpu.VMEM((L + 2 * HALO, LANES), F32), seq, seq],
        compiler_params=_cparams(3),
        name="hyena_prep",
    )(proj, proj, conv_w, conv_b.reshape(1, 3 * width))


def _unpack_kernel(x_ref, o_ref):
    n2_count, n1v, _ = x_ref.shape

    def per_n2(n2, carry):
        lo, hi = _unpack(x_ref[n2])
        o_ref[0, pl.ds(n2, n1v, stride=n2_count), :] = lo
        o_ref[1, pl.ds(n2, n1v, stride=n2_count), :] = hi
        return carry

    lax.fori_loop(0, n2_count, per_n2, 0, unroll=4)


def unpack_pairs(z, *, L):
    P, chunks, N2, n1v, _ = z.shape
    assert n1v * N2 >= L
    out = pl.pallas_call(
        _unpack_kernel,
        grid=(P, chunks),
        in_specs=[pl.BlockSpec((None, None, N2, n1v, LANES), lambda p, c: (p, c, 0, 0, 0))],
        out_specs=pl.BlockSpec((None, 2, None, n1v * N2, LANES), lambda p, c: (p, 0, c, 0, 0)),
        out_shape=jax.ShapeDtypeStruct((P, 2, chunks, n1v * N2, LANES), F32),
        compiler_params=_cparams(2),
        name="unpack_pairs",
    )(z)
    return out.reshape(2 * P, chunks, n1v * N2, LANES)


FILTER_N2 = 4


def _filter_kernel(bands_ref, deltas_ref, w1t_ref, w1c_ref, w1s_ref, b1_ref, w2_ref, b2_ref,
                   w3_ref, b3_ref, fr_ref, w4_ref, skip_ref, o_ref, *, L, N, C):
    n1 = o_ref.shape[3]
    rows = FILTER_N2 * n1
    n2_base = pl.program_id(0) * FILTER_N2
    dot_hi = functools.partial(jnp.dot, precision=HI, preferred_element_type=F32)

    def cyclic_index(shape, axis):
        r = lax.broadcasted_iota(jnp.int32, shape, axis)
        j = sum((r >= k * n1).astype(jnp.int32) for k in range(1, FILTER_N2))
        return (r - j * n1) * FFT_N2 + n2_base + j

    m_row = cyclic_index((1, rows), 1)
    pos_row = jnp.where(m_row < L, m_row, N - m_row).astype(F32)
    t_row = pos_row / (L - 1)
    ang = bands_ref[...] * ((2.0 * math.pi / L) * pos_row)
    fr = fr_ref[...]
    pre = (w1t_ref[...] * t_row + dot_hi(w1c_ref[...], jnp.cos(ang))
           + dot_hi(w1s_ref[...], -jnp.sin(ang)) + b1_ref[...])
    h = jnp.sin(fr * pre)
    h = jnp.sin(fr * (dot_hi(w2_ref[...], h) + b2_ref[...]))
    h = jnp.sin(fr * (dot_hi(w3_ref[...], h) + b3_ref[...]))

    m = cyclic_index((rows, 1), 0)
    fwd = m < L
    bwd = m > N - L
    t = jnp.where(fwd, m, N - m).astype(F32) / (L - 1)
    decay = jnp.exp(-t * deltas_ref[...])
    taps_of = lambda col: lax.dot_general(
        h, w4_ref[:, col * C:(col + 1) * C], (((0,), (0,)), ((), ())),
        preferred_element_type=F32)
    for order in range(2):
        k = jnp.where(fwd, taps_of(2 * order), jnp.where(bwd, taps_of(2 * order + 1), 0.0)) * decay
        k = jnp.where(m == 0, k + skip_ref[order:order + 1, :], k)
        for c, kc in enumerate(_lane_chunks(k)):
            for j in range(FILTER_N2):
                o_ref[order, c, j] = kc[j * n1:(j + 1) * n1]


def hyena_filter_taps(w1, b1, w2, b2, w3, b3, freq, w4, skip, *, L, N1):
    C = skip.shape[1]
    fw = w2.shape[0]
    N2 = FFT_N2
    bands = jnp.linspace(1e-4, FILTER_BANDS - 1, FILTER_BANDS, dtype=F32).reshape(FILTER_BANDS, 1)
    deltas = jnp.abs(jnp.linspace(math.log(DECAY_TARGET) / SLOW_DECAY_PCT,
                                  math.log(DECAY_TARGET) / FAST_DECAY_PCT, C, dtype=F32)).reshape(1, C)
    col = lambda a: a.reshape(fw, 1)
    args = (bands, deltas, w1[0:1].T, w1[1:1 + FILTER_BANDS].T, w1[1 + FILTER_BANDS:].T,
            col(b1), w2.T, col(b2), w3.T, col(b3), col(freq), w4, skip)
    small = lambda a: pl.BlockSpec(a.shape, lambda i: (0,) * a.ndim)
    return pl.pallas_call(
        functools.partial(_filter_kernel, L=L, N=N1 * N2, C=C),
        grid=(N2 // FILTER_N2,),
        in_specs=[small(a) for a in args],
        out_specs=pl.BlockSpec((2, C // LANES, FILTER_N2, N1, LANES), lambda i: (0, 0, i, 0, 0)),
        out_shape=jax.ShapeDtypeStruct((2, C // LANES, N2, N1, LANES), F32),
        compiler_params=_cparams(1),
        name="hyena_filter_taps",
    )(*args)


def _real_form(m):
    return np.block([[m.real, -m.imag], [m.imag, m.real]])


def _interleave(n):
    return np.arange(2 * n).reshape(2, n).T.reshape(-1)


@functools.lru_cache(maxsize=None)
def _dft_tables(N1, N2):
    N = N1 * N2
    k1 = np.arange(N1)
    g1 = np.exp(-2j * np.pi * ((k1[:, None] * k1[None, :]) % N1) / N1)
    il1, il2 = _interleave(N1), _interleave(N2)
    stage1 = _real_form(g1)[il1][:, il1]
    stage1_real = _real_form(g1)[il1][:, :N1]
    stage1_inv = _real_form(np.conj(g1))[il1][:, il1]
    n2 = np.arange(N2)
    k = k1[:, None, None] + N1 * n2[None, :, None]
    mk = np.exp(-2j * np.pi * ((k * n2[None, None, :]) % N) / N)
    fwd = np.stack([_real_form(mk[i]) for i in range(N1)])
    stage2 = fwd[:, :, il2]
    stage2_il = fwd[:, il2][:, :, il2]
    stage2_inv = np.stack([_real_form(np.conj(mk[i]).T) for i in range(N1)])[:, il2]
    return tuple(a.astype(BF16) for a in
                 (stage1, stage1_real, stage1_inv, stage2, stage2_il, stage2_inv))


STAGE1_J = 16
STAGE1_CHUNKS = 4


STAGE2_K1 = 8


def _time_spec(jb, cb, n1, lead):
    if lead is None:
        return pl.BlockSpec((None, cb, jb, n1, LANES), lambda p, j, c: (p, c, j, 0, 0))
    return pl.BlockSpec((None, None, cb, jb, n1, LANES), lambda p, j, c: (lead, p, c, j, 0, 0))


def _freq_spec_n2(jb, cb, n1):
    kb = STAGE2_K1
    return pl.BlockSpec((None, cb, n1 // kb, jb * kb, LANES), lambda p, j, c: (p, c, 0, j, 0))


def _load_time(x_ref, j):
    return jnp.concatenate([x_ref[c, j] for c in range(x_ref.shape[0])], axis=1)


def _load_freq(x_ref, j):
    kb = STAGE2_K1
    rows = slice(j * kb, (j + 1) * kb)
    return jnp.concatenate(
        [jnp.concatenate([x_ref[c, k, rows, :] for k in range(x_ref.shape[1])], axis=0)
         for c in range(x_ref.shape[0])], axis=1)


def _store_time(o_ref, j, word):
    for c, w in enumerate(_lane_chunks(word)):
        o_ref[c, j] = w


def _store_freq(o_ref, j, word):
    kb = STAGE2_K1
    for c, w in enumerate(_lane_chunks(word)):
        for k in range(o_ref.shape[1]):
            o_ref[c, k, j * kb:(j + 1) * kb, :] = w[k * kb:(k + 1) * kb]


def _stage1_matmul(m_ref, word, real_input):
    operand = word.astype(BF16) if real_input else _words_to_bf16(word)
    return jnp.dot(m_ref[...], operand, preferred_element_type=F32)


def _stage1_kernel(m_ref, x_ref, o_ref, *, real_input):
    for j in range(STAGE1_J):
        y = _stage1_matmul(m_ref, _load_time(x_ref, j), real_input)
        _store_freq(o_ref, j, _f32_to_words(y))


def dft_stage1(mat, x, lead=None):
    P, chunks, N2, n1x, _ = x.shape[-5:]
    N1 = mat.shape[0] // 2
    assert mat.shape[1] == (n1x if x.dtype == F32 else 2 * n1x)
    jb, cb, kb = STAGE1_J, STAGE1_CHUNKS, STAGE2_K1
    return pl.pallas_call(
        functools.partial(_stage1_kernel, real_input=(x.dtype == F32)),
        grid=(P, N2 // jb, chunks // cb),
        in_specs=[pl.BlockSpec(mat.shape, lambda p, j, c: (0, 0)),
                  _time_spec(jb, cb, n1x, lead)],
        out_specs=_freq_spec_n2(jb, cb, N1),
        out_shape=jax.ShapeDtypeStruct((P, chunks, N1 // kb, N2 * kb, LANES), U32),
        compiler_params=_cparams(3),
        name="dft_stage1",
    )(mat, x)


def _stage1_gate_kernel(m_ref, x_ref, gate_ref, o_ref):
    for j in range(STAGE1_J):
        y = _stage1_matmul(m_ref, _load_freq(x_ref, j), False)
        gate = _words_to_bf16(_load_time(gate_ref, j)).astype(F32)
        _store_time(o_ref, j, _f32_to_words(y * gate))


def dft_stage1_inverse_gate(mat, x, gates, lead):
    P, chunks, n_kb, _, _ = x.shape
    N2, n1g = gates.shape[-3], gates.shape[-2]
    N1 = n_kb * STAGE2_K1
    assert mat.shape == (2 * n1g, 2 * N1)
    jb, cb = STAGE1_J, STAGE1_CHUNKS
    return pl.pallas_call(
        _stage1_gate_kernel,
        grid=(P, N2 // jb, chunks // cb),
        in_specs=[pl.BlockSpec(mat.shape, lambda p, j, c: (0, 0)),
                  _freq_spec_n2(jb, cb, N1),
                  _time_spec(jb, cb, n1g, lead)],
        out_specs=_time_spec(jb, cb, n1g, None),
        out_shape=jax.ShapeDtypeStruct((P, chunks, N2, n1g, LANES), U32),
        compiler_params=_cparams(3),
        name="dft_stage1_inverse_gate",
    )(mat, x, gates)


def _stage1_gate_stage1_kernel(mi_ref, mf_ref, x_ref, gate_ref, o_ref, z_ref):
    for j in range(STAGE1_J):
        y = _stage1_matmul(mi_ref, _load_freq(x_ref, j), False)
        gate = _words_to_bf16(_load_time(gate_ref, j)).astype(F32)
        z_ref[j] = (y * gate).astype(BF16)
    for j in range(STAGE1_J):
        a = jnp.dot(mf_ref[...], z_ref[j], preferred_element_type=F32)
        _store_freq(o_ref, j, _f32_to_words(a))


def dft_stage1_inverse_gate_forward(mat_inv, mat_fwd, x, gates, lead):
    P, chunks, n_kb, _, _ = x.shape
    N2, n1g = gates.shape[-3], gates.shape[-2]
    N1 = n_kb * STAGE2_K1
    assert mat_inv.shape == (2 * n1g, 2 * N1) and mat_fwd.shape == (2 * N1, 2 * n1g)
    jb, cb = STAGE1_J, STAGE1_CHUNKS
    return pl.pallas_call(
        _stage1_gate_stage1_kernel,
        grid=(P, N2 // jb, chunks // cb),
        in_specs=[pl.BlockSpec(mat_inv.shape, lambda p, j, c: (0, 0)),
                  pl.BlockSpec(mat_fwd.shape, lambda p, j, c: (0, 0)),
                  _freq_spec_n2(jb, cb, N1),
                  _time_spec(jb, cb, n1g, lead)],
        out_specs=_freq_spec_n2(jb, cb, N1),
        out_shape=jax.ShapeDtypeStruct(x.shape, U32),
        scratch_shapes=[pltpu.VMEM((jb, 2 * n1g, cb * LANES), BF16)],
        compiler_params=_cparams(3),
        name="dft_stage1_inverse_gate_forward",
    )(mat_inv, mat_fwd, x, gates)


def _freq_spec_k1(chunks, n2):
    kb = STAGE2_K1
    return pl.BlockSpec((None, chunks, None, n2 * kb, LANES), lambda p, k: (p, 0, k, 0, 0))


def _stage2_rows(ref, i):
    return pl.ds(i, ref.shape[1] // STAGE2_K1, stride=STAGE2_K1)


def _stage2_operand(a_ref, i):
    rows = _stage2_rows(a_ref, i)
    word = jnp.concatenate([a_ref[c, rows, :] for c in range(a_ref.shape[0])], axis=1)
    return _words_to_bf16(word)


def _stage2_spectrum_kernel(m_ref, a_ref, o_ref, *, scale):
    for i in range(STAGE2_K1):
        x = jnp.dot(m_ref[i], _stage2_operand(a_ref, i), preferred_element_type=F32) * scale
        o_ref[i] = _f32_to_words(x)


def filter_spectrum(stage2, a, *, N2):
    kb = STAGE2_K1
    _, chunks, n_kb, _, _ = a.shape
    N1 = n_kb * kb
    C = chunks * LANES
    return pl.pallas_call(
        functools.partial(_stage2_spectrum_kernel, scale=1.0 / (N1 * N2)),
        grid=(2, n_kb),
        in_specs=[pl.BlockSpec((kb, 2 * N2, 2 * N2), lambda o, k: (k, 0, 0)),
                  _freq_spec_k1(chunks, N2)],
        out_specs=pl.BlockSpec((None, kb, N2, C), lambda o, k: (o, k, 0, 0)),
        out_shape=jax.ShapeDtypeStruct((2, N1, N2, C), U32),
        compiler_params=_cparams(2),
        name="filter_spectrum",
    )(stage2, a)


def _stage2_conv_kernel(mf_ref, mi_ref, a_ref, k_ref, o_ref, y_ref):
    n2 = k_ref.shape[1]
    for i in range(STAGE2_K1):
        x = jnp.dot(mf_ref[i], _stage2_operand(a_ref, i), preferred_element_type=F32)
        xr, xi = x[:n2], x[n2:]
        kr, ki = _unpack(k_ref[i])
        y_ref[i] = jnp.concatenate([xr * kr - xi * ki, xr * ki + xi * kr], axis=0).astype(BF16)
    for i in range(STAGE2_K1):
        b = jnp.dot(mi_ref[i], y_ref[i], preferred_element_type=F32)
        rows = _stage2_rows(o_ref, i)
        for c, word in enumerate(_lane_chunks(_f32_to_words(b))):
            o_ref[c, rows, :] = word


def dft_stage2_conv(stage2, stage2_inv, a, kspec, order):
    kb = STAGE2_K1
    P, chunks, n_kb, _, _ = a.shape
    N2, C = kspec.shape[2], kspec.shape[3]
    seq = pl.BlockSpec((None, chunks, None, N2 * kb, LANES), lambda k, p: (p, 0, k, 0, 0))
    return pl.pallas_call(
        _stage2_conv_kernel,
        grid=(n_kb, P),
        in_specs=[pl.BlockSpec((kb, 2 * N2, 2 * N2), lambda k, p: (k, 0, 0)),
                  pl.BlockSpec((kb, 2 * N2, 2 * N2), lambda k, p: (k, 0, 0)),
                  seq,
                  pl.BlockSpec((None, kb, N2, C), lambda k, p: (order, k, 0, 0))],
        out_specs=seq,
        out_shape=jax.ShapeDtypeStruct(a.shape, U32),
        scratch_shapes=[pltpu.VMEM((kb, 2 * N2, C), BF16)],
        compiler_params=_cparams(2),
        name="dft_stage2_conv",
    )(stage2, stage2_inv, a, kspec)


def hyena_mixer(proj, conv_w, conv_b, w1, b1, w2, b2, w3, b3, freq, w4, skip, *, B, L, col0):
    C = skip.shape[1]
    N1, N2 = _fft_factor(L), FFT_N2
    stage1, stage1_real, stage1_inv, stage2, stage2_il, stage2_inv = _dft_tables(N1, N2)

    taps = hyena_filter_taps(w1, b1, w2, b2, w3, b3, freq, w4, skip, L=L, N1=N1)
    kspec = filter_spectrum(stage2_il, dft_stage1(stage1_real, taps), N2=N2)

    streams = hyena_prep(proj, conv_w, conv_b, B=B, L=L, col0=col0, width=C)
    valid = 2 * _valid_n1(L)
    a = dft_stage1(stage1[:, :valid], streams, lead=0)
    b = dft_stage2_conv(stage2, stage2_inv, a, kspec, 0)
    a = dft_stage1_inverse_gate_forward(stage1_inv[:valid], stage1[:, :valid], b, streams, 1)
    b = dft_stage2_conv(stage2, stage2_inv, a, kspec, 1)
    z = dft_stage1_inverse_gate(stage1_inv[:valid], b, streams, 2)
    return unpack_pairs(z, L=L)


def _trunk(x, meta_tokens, norm1_g, in_proj, pool_w, pool_scale, hy_conv_w, hy_conv_b,
           flt_w1, flt_b1, flt_w2, flt_b2, flt_w3, flt_b3, flt_freq, flt_w4, hy_skip,
           out_proj, norm2_g, w_gate, w_up, w_down, final_g):
    B, S, D = x.shape
    L = S + N_META
    T = B * L
    depth = in_proj.shape[0]
    pool_width = pool_scale.shape[1]
    meta = jnp.broadcast_to(meta_tokens[None].astype(x.dtype), (B, N_META, D))
    h = jnp.concatenate([meta, x], axis=1).reshape(T, D)
    xn = norm_rows(h, norm1_g[0])
    for l in range(depth):
        proj = matmul_resident(xn, in_proj, l).reshape(B, L, in_proj.shape[2])
        a = pool_mixer(proj, pool_w[l], pool_scale[l], B=B, L=L)
        b = hyena_mixer(proj, hy_conv_w[l], hy_conv_b[l], flt_w1[l], flt_b1[l], flt_w2[l],
                        flt_b2[l], flt_w3[l], flt_b3[l], flt_freq[l], flt_w4[l], hy_skip[l],
                        B=B, L=L, col0=pool_width)
        h, xn = mixer_residual_norm(a, b, out_proj, l, h, norm2_g[l], B=B, L=L)
        act = swiglu(xn, w_gate, w_up, l, tn=512)
        if l + 1 < depth:
            h, xn = matmul_residual_norm(act, w_down, l, h, norm1_g[l + 1])
    return matmul_residual_final(act, w_down, depth - 1, h, final_g, B=B, L=L, skip=N_META)


def kernel(x_prompt, x_sample, meta_tokens, norm1_g, in_proj, pool_w, pool_scale, hy_conv_w, hy_conv_b, flt_w1, flt_b1, flt_w2, flt_b2, flt_w3, flt_b3, flt_freq, flt_w4, hy_skip, out_proj, norm2_g, w_gate, w_up, w_down, final_g):
    params = (meta_tokens, norm1_g, in_proj.astype(BF16), pool_w.astype(BF16), pool_scale,
              hy_conv_w, hy_conv_b, flt_w1, flt_b1, flt_w2, flt_b2, flt_w3, flt_b3, flt_freq,
              flt_w4, hy_skip, out_proj.astype(BF16), norm2_g, w_gate.astype(BF16),
              w_up.astype(BF16), w_down.astype(BF16), final_g)
    return (_trunk(x_prompt, *params), _trunk(x_sample, *params))
```

```python
import functools
import math

import numpy as np
import jax
import jax.numpy as jnp
from jax import lax
from jax.experimental import pallas as pl
from jax.experimental.pallas import tpu as pltpu

F32 = jnp.float32
BF16 = jnp.bfloat16
U32 = jnp.uint32

EPS = 1e-6
N_META = 16
POOL_WINDOWS = (2, 4, 8, 16)
POOL_GROUP = 256
FILTER_BANDS = 16
DECAY_TARGET = 1e-2
FAST_DECAY_PCT = 0.3
SLOW_DECAY_PCT = 1.5

LANES = 128
HALO = 8
FFT_N2 = 80
ROWS_BIG = 1040
ROWS_SMALL = 272
ROWS_FINAL = 256
VMEM_LIMIT = 56 * 1024 * 1024
HI = lax.Precision.HIGHEST


def _cparams(n_axes):
    return pltpu.CompilerParams(
        dimension_semantics=("arbitrary",) * n_axes,
        vmem_limit_bytes=VMEM_LIMIT)


def _resident_layer(w, layer):
    return pl.BlockSpec((None,) + w.shape[1:], lambda *_: (layer, 0, 0),
                        pipeline_mode=pl.Buffered(1))


def _unpack(word):
    lo = lax.bitcast_convert_type(word << 16, F32)
    hi = lax.bitcast_convert_type(word & jnp.uint32(0xFFFF0000), F32)
    return lo, hi


def _pack(lo, hi):
    lo = lax.bitcast_convert_type(lo.astype(BF16).astype(F32), U32)
    hi = lax.bitcast_convert_type(hi.astype(BF16).astype(F32), U32)
    return hi | (lo >> 16)


def _words_to_bf16(word):
    return pltpu.bitcast(word, BF16)


def _f32_to_words(x):
    return pltpu.bitcast(x.astype(BF16), U32)


def _lane_chunks(x):
    return [x[:, c * LANES:(c + 1) * LANES] for c in range(x.shape[1] // LANES)]


def _rmsnorm_rows(x, g):
    ms = jnp.mean(x * x, axis=-1, keepdims=True)
    return x * lax.rsqrt(ms + EPS) * g


def _embed_norm_kernel(x_ref, tail_ref, meta_ref, g_ref, h_ref, xn_ref):
    n_meta = meta_ref.shape[0]
    head = jnp.where(pl.program_id(1) == 0, meta_ref[...], tail_ref[...])
    h = jnp.concatenate([head, x_ref[0:x_ref.shape[0] - n_meta, :]], axis=0)
    h_ref[...] = h
    xn_ref[...] = _rmsnorm_rows(h, g_ref[...]).astype(xn_ref.dtype)


def embed_norm(x, meta, g):
    B, S, D = x.shape
    n_meta = meta.shape[0]
    L = S + n_meta
    tm = ROWS_BIG
    assert tm % n_meta == 0 and n_meta % 8 == 0
    tail_blocks = tm // n_meta
    h, xn = pl.pallas_call(
        _embed_norm_kernel,
        grid=(B, pl.cdiv(L, tm)),
        in_specs=[pl.BlockSpec((None, tm, D), lambda b, i: (b, i, 0)),
                  pl.BlockSpec((None, n_meta, D),
                               lambda b, i: (b, jnp.maximum(i * tail_blocks - 1, 0), 0)),
                  pl.BlockSpec((n_meta, D), lambda b, i: (0, 0)),
                  pl.BlockSpec((1, D), lambda b, i: (0, 0))],
        out_specs=[pl.BlockSpec((None, tm, D), lambda b, i: (b, i, 0)),
                   pl.BlockSpec((None, tm, D), lambda b, i: (b, i, 0))],
        out_shape=[jax.ShapeDtypeStruct((B, L, D), F32), jax.ShapeDtypeStruct((B, L, D), BF16)],
        compiler_params=_cparams(2),
        name="embed_norm",
    )(x, x, meta, g.reshape(1, D))
    return h.reshape(B * L, D), xn.reshape(B * L, D)


MATMUL_COLS = 1024


def _matmul_kernel(x_ref, w_ref, o_ref):
    x = x_ref[...]
    for c in range(0, o_ref.shape[1], MATMUL_COLS):
        o_ref[:, c:c + MATMUL_COLS] = jnp.dot(
            x, w_ref[:, c:c + MATMUL_COLS], preferred_element_type=F32).astype(o_ref.dtype)


def matmul_resident(x, w, layer):
    T, K = x.shape
    n_out = w.shape[2]
    tm = ROWS_BIG
    return pl.pallas_call(
        _matmul_kernel,
        grid=(pl.cdiv(T, tm),),
        in_specs=[pl.BlockSpec((tm, K), lambda i: (i, 0)), _resident_layer(w, layer)],
        out_specs=pl.BlockSpec((tm, n_out), lambda i: (i, 0)),
        out_shape=jax.ShapeDtypeStruct((T, n_out), BF16),
        compiler_params=_cparams(1),
        name="matmul_resident",
    )(x, w)


def _swiglu_kernel(x_ref, wg_ref, wu_ref, o_ref):
    x = x_ref[...]
    gate = jnp.dot(x, wg_ref[...], preferred_element_type=F32)
    up = jnp.dot(x, wu_ref[...], preferred_element_type=F32)
    o_ref[...] = (gate * jax.nn.sigmoid(gate) * up).astype(o_ref.dtype)


def swiglu(x, wg, wu, layer, *, tn):
    T, K = x.shape
    n_out = wg.shape[2]
    tm = ROWS_BIG
    return pl.pallas_call(
        _swiglu_kernel,
        grid=(n_out // tn, pl.cdiv(T, tm)),
        in_specs=[pl.BlockSpec((tm, K), lambda j, i: (i, 0)),
                  pl.BlockSpec((None, K, tn), lambda j, i: (layer, 0, j)),
                  pl.BlockSpec((None, K, tn), lambda j, i: (layer, 0, j))],
        out_specs=pl.BlockSpec((tm, tn), lambda j, i: (i, j)),
        out_shape=jax.ShapeDtypeStruct((T, n_out), BF16),
        compiler_params=_cparams(2),
        name="swiglu",
    )(x, wg, wu)


def _mixer_residual_norm_kernel(a_ref, b_ref, w_ref, r_ref, g_ref, h_ref, xn_ref):
    ka = a_ref.shape[1]
    b = jnp.concatenate([b_ref[c] for c in range(b_ref.shape[0])], axis=1).astype(BF16)
    acc = (r_ref[...] + jnp.dot(a_ref[...], w_ref[0:ka, :], preferred_element_type=F32)
           + jnp.dot(b, w_ref[ka:, :], preferred_element_type=F32))
    h_ref[...] = acc
    xn_ref[...] = _rmsnorm_rows(acc, g_ref[...]).astype(xn_ref.dtype)


def mixer_residual_norm(a, b, w, layer, res, g, *, B, L):
    D = res.shape[1]
    ka = a.shape[2]
    chunks = b.shape[1]
    tm = ROWS_SMALL
    h, xn = pl.pallas_call(
        _mixer_residual_norm_kernel,
        grid=(B, pl.cdiv(L, tm)),
        in_specs=[pl.BlockSpec((None, tm, ka), lambda s, i: (s, i, 0)),
                  pl.BlockSpec((None, chunks, tm, LANES), lambda s, i: (s, 0, i, 0)),
                  _resident_layer(w, layer),
                  pl.BlockSpec((None, tm, D), lambda s, i: (s, i, 0)),
                  pl.BlockSpec((1, D), lambda s, i: (0, 0))],
        out_specs=[pl.BlockSpec((None, tm, D), lambda s, i: (s, i, 0)),
                   pl.BlockSpec((None, tm, D), lambda s, i: (s, i, 0))],
        out_shape=[jax.ShapeDtypeStruct((B, L, D), F32), jax.ShapeDtypeStruct((B, L, D), BF16)],
        compiler_params=_cparams(2),
        name="mixer_residual_norm",
    )(a, b, w, res.reshape(B, L, D), g.reshape(1, D))
    return h.reshape(B * L, D), xn.reshape(B * L, D)


def _residual_norm_kernel(x_ref, w_ref, r_ref, g_ref, h_ref, xn_ref):
    acc = r_ref[...] + jnp.dot(x_ref[...], w_ref[...], preferred_element_type=F32)
    h_ref[...] = acc
    xn_ref[...] = _rmsnorm_rows(acc, g_ref[...]).astype(xn_ref.dtype)


def matmul_residual_norm(x, w, layer, res, g):
    T, D = res.shape
    tm = ROWS_SMALL
    return pl.pallas_call(
        _residual_norm_kernel,
        grid=(pl.cdiv(T, tm),),
        in_specs=[pl.BlockSpec((tm, x.shape[1]), lambda i: (i, 0)),
                  _resident_layer(w, layer),
                  pl.BlockSpec((tm, D), lambda i: (i, 0)),
                  pl.BlockSpec((1, D), lambda i: (0, 0))],
        out_specs=[pl.BlockSpec((tm, D), lambda i: (i, 0)),
                   pl.BlockSpec((tm, D), lambda i: (i, 0))],
        out_shape=[jax.ShapeDtypeStruct((T, D), F32), jax.ShapeDtypeStruct((T, D), BF16)],
        compiler_params=_cparams(1),
        name="matmul_residual_norm",
    )(x, w, res, g.reshape(1, D))


def _residual_final_kernel(x_ref, w_ref, r_ref, g_ref, y_ref):
    acc = r_ref[...] + jnp.dot(x_ref[...], w_ref[...], preferred_element_type=F32)
    y_ref[...] = _rmsnorm_rows(acc, g_ref[...])


def matmul_residual_final(x, w, layer, res, g, *, B, L, skip):
    K = x.shape[1]
    D = res.shape[1]
    S = L - skip
    tm = ROWS_FINAL
    assert L % 16 == 0 and skip % 16 == 0 and tm % 16 == 0
    rows = lambda b, i: pl.multiple_of(b * L + skip + i * tm, 16)
    return pl.pallas_call(
        _residual_final_kernel,
        grid=(B, S // tm),
        in_specs=[pl.BlockSpec((pl.Element(tm), pl.Element(K)), lambda b, i: (rows(b, i), 0)),
                  _resident_layer(w, layer),
                  pl.BlockSpec((pl.Element(tm), pl.Element(D)), lambda b, i: (rows(b, i), 0)),
                  pl.BlockSpec((1, D), lambda b, i: (0, 0))],
        out_specs=pl.BlockSpec((None, tm, D), lambda b, i: (b, i, 0)),
        out_shape=jax.ShapeDtypeStruct((B, S, D), F32),
        compiler_params=_cparams(2),
        name="matmul_residual_final",
    )(x, w, res, g.reshape(1, D))


def _fill_padded(pad_ref, x_ref, L):
    rows, width = pad_ref.shape
    pad_ref[0:HALO, :] = jnp.zeros((HALO, width), F32)
    pad_ref[HALO + L:rows, :] = jnp.zeros((rows - HALO - L, width), F32)
    pad_ref[HALO:HALO + L, :] = x_ref[...].astype(F32)


def _chunk_start(k, rows, L):
    return pl.multiple_of(jnp.minimum(k * rows, L - rows), 16)


POOL_CHUNK = 512


def _pool_kernel(u_ref, w_ref, s_ref, o_ref, pad_ref, *, L):
    g = pl.program_id(1)
    _fill_padded(pad_ref, u_ref, L)
    rows = POOL_CHUNK
    n_chunks = pl.cdiv(L, rows)

    def run(window):
        hw = window // 2

        def chunk(k, carry):
            start = _chunk_start(k, rows, L)
            win = pad_ref[pl.ds(start, rows + 2 * HALO), :]
            s, w, off = win[:-1] + win[1:], 2, 1
            while w < window:
                s, w, off = s[:-w] + s[w:], 2 * w, off + w // 2
            total = s[HALO - off:HALO - off + rows]
            t = start + lax.broadcasted_iota(jnp.int32, (rows, POOL_GROUP), 0)
            cnt = (jnp.minimum(t + hw, L) - jnp.maximum(t - hw, 0)).astype(F32)
            d = total / cnt - win[HALO:HALO + rows]
            y = jnp.dot(d.astype(BF16), w_ref[...], preferred_element_type=F32)
            o_ref[pl.ds(start, rows), :] = (y * s_ref[...]).astype(o_ref.dtype)
            return carry

        lax.fori_loop(0, n_chunks, chunk, 0)

    for gi, window in enumerate(POOL_WINDOWS):
        pl.when(g == gi)(functools.partial(run, window))


def pool_mixer(proj, pool_w, pool_scale, *, B, L):
    n_groups = len(POOL_WINDOWS)
    width = n_groups * POOL_GROUP
    return pl.pallas_call(
        functools.partial(_pool_kernel, L=L),
        grid=(B, n_groups),
        in_specs=[pl.BlockSpec((None, L, POOL_GROUP), lambda b, g: (b, 0, g)),
                  pl.BlockSpec((None, POOL_GROUP, POOL_GROUP), lambda b, g: (g, 0, 0)),
                  pl.BlockSpec((1, POOL_GROUP), lambda b, g: (0, g))],
        out_specs=pl.BlockSpec((None, L, POOL_GROUP), lambda b, g: (b, 0, g)),
        out_shape=jax.ShapeDtypeStruct((B, L, width), BF16),
        scratch_shapes=[pltpu.VMEM((L + 2 * HALO, POOL_GROUP), F32)],
        compiler_params=_cparams(2),
        name="pool_mixer",
    )(proj, pool_w, pool_scale.reshape(1, width))


def _fft_factor(L):
    n1 = -(-(2 * L - 1) // FFT_N2)
    return -(-n1 // STAGE2_K1) * STAGE2_K1


def _valid_n1(L):
    return -(-(-(-L // FFT_N2)) // 8) * 8


PREP_CHUNK = 512


def _prep_kernel(u0_ref, u1_ref, w_ref, b_ref, o_ref, pad_ref, y0_ref, y1_ref, *, L):
    n2_count, n1v, _ = o_ref.shape
    rows = PREP_CHUNK
    w0, w1, w2 = w_ref[0:1, :], w_ref[1:2, :], w_ref[2:3, :]
    bias = b_ref[...]

    for u_ref, y_ref in ((u0_ref, y0_ref), (u1_ref, y1_ref)):
        _fill_padded(pad_ref, u_ref, L)
        y_ref[L:y_ref.shape[0], :] = jnp.zeros((y_ref.shape[0] - L, LANES), F32)

        def chunk(k, carry, y_ref=y_ref):
            start = _chunk_start(k, rows, L)
            win = pad_ref[pl.ds(start, rows + 2 * HALO), :]
            y_ref[pl.ds(start, rows), :] = (
                win[HALO - 1:HALO - 1 + rows] * w0 + win[HALO:HALO + rows] * w1
                + win[HALO + 1:HALO + 1 + rows] * w2 + bias)
            return carry

        lax.fori_loop(0, pl.cdiv(L, rows), chunk, 0)

    def per_n2(n2, carry):
        rows_n2 = pl.ds(n2, n1v, stride=n2_count)
        o_ref[n2] = _pack(y0_ref[rows_n2, :], y1_ref[rows_n2, :])
        return carry

    lax.fori_loop(0, n2_count, per_n2, 0, unroll=4)


def hyena_prep(proj, conv_w, conv_b, *, B, L, col0, width):
    N2 = FFT_N2
    n_cb = width // LANES
    n1v = _valid_n1(L)
    assert col0 % LANES == 0
    cols = lambda s, c: col0 // LANES + s * n_cb + c
    seq = pltpu.VMEM((n1v * N2, LANES), F32)
    return pl.pallas_call(
        functools.partial(_prep_kernel, L=L),
        grid=(3, B // 2, n_cb),
        in_specs=[pl.BlockSpec((None, L, LANES), lambda s, p, c: (2 * p, 0, cols(s, c))),
                  pl.BlockSpec((None, L, LANES), lambda s, p, c: (2 * p + 1, 0, cols(s, c))),
                  pl.BlockSpec((3, LANES), lambda s, p, c: (0, s * n_cb + c)),
                  pl.BlockSpec((1, LANES), lambda s, p, c: (0, s * n_cb + c))],
        out_specs=pl.BlockSpec((None, None, None, N2, n1v, LANES),
                               lambda s, p, c: (s, p, c, 0, 0, 0)),
        out_shape=jax.ShapeDtypeStruct((3, B // 2, n_cb, N2, n1v, LANES), U32),
        scratch_shapes=[pltpu.VMEM((L + 2 * HALO, LANES), F32), seq, seq],
        compiler_params=_cparams(3),
        name="hyena_prep",
    )(proj, proj, conv_w, conv_b.reshape(1, 3 * width))


def _unpack_kernel(x_ref, o_ref):
    n2_count, n1v, _ = x_ref.shape

    def per_n2(n2, carry):
        lo, hi = _unpack(x_ref[n2])
        o_ref[0, pl.ds(n2, n1v, stride=n2_count), :] = lo
        o_ref[1, pl.ds(n2, n1v, stride=n2_count), :] = hi
        return carry

    lax.fori_loop(0, n2_count, per_n2, 0, unroll=4)


def unpack_pairs(z, *, L):
    P, chunks, N2, n1v, _ = z.shape
    assert n1v * N2 >= L
    out = pl.pallas_call(
        _unpack_kernel,
        grid=(P, chunks),
        in_specs=[pl.BlockSpec((None, None, N2, n1v, LANES), lambda p, c: (p, c, 0, 0, 0))],
        out_specs=pl.BlockSpec((None, 2, None, n1v * N2, LANES), lambda p, c: (p, 0, c, 0, 0)),
        out_shape=jax.ShapeDtypeStruct((P, 2, chunks, n1v * N2, LANES), F32),
        compiler_params=_cparams(2),
        name="unpack_pairs",
    )(z)
    return out.reshape(2 * P, chunks, n1v * N2, LANES)


FILTER_N2 = 4


def _filter_kernel(bands_ref, deltas_ref, w1t_ref, w1c_ref, w1s_ref, b1_ref, w2_ref, b2_ref,
                   w3_ref, b3_ref, fr_ref, w4_ref, skip_ref, o_ref, *, L, N, C):
    n1 = o_ref.shape[3]
    rows = FILTER_N2 * n1
    n2_base = pl.program_id(0) * FILTER_N2
    dot_hi = functools.partial(jnp.dot, precision=HI, preferred_element_type=F32)

    def cyclic_index(shape, axis):
        r = lax.broadcasted_iota(jnp.int32, shape, axis)
        j = sum((r >= k * n1).astype(jnp.int32) for k in range(1, FILTER_N2))
        return (r - j * n1) * FFT_N2 + n2_base + j

    m_row = cyclic_index((1, rows), 1)
    pos_row = jnp.where(m_row < L, m_row, N - m_row).astype(F32)
    t_row = pos_row / (L - 1)
    ang = bands_ref[...] * ((2.0 * math.pi / L) * pos_row)
    fr = fr_ref[...]
    pre = (w1t_ref[...] * t_row + dot_hi(w1c_ref[...], jnp.cos(ang))
           + dot_hi(w1s_ref[...], -jnp.sin(ang)) + b1_ref[...])
    h = jnp.sin(fr * pre)
    h = jnp.sin(fr * (dot_hi(w2_ref[...], h) + b2_ref[...]))
    h = jnp.sin(fr * (dot_hi(w3_ref[...], h) + b3_ref[...]))

    m = cyclic_index((rows, 1), 0)
    fwd = m < L
    bwd = m > N - L
    t = jnp.where(fwd, m, N - m).astype(F32) / (L - 1)
    decay = jnp.exp(-t * deltas_ref[...])
    taps_of = lambda col: lax.dot_general(
        h, w4_ref[:, col * C:(col + 1) * C], (((0,), (0,)), ((), ())),
        preferred_element_type=F32)
    for order in range(2):
        k = jnp.where(fwd, taps_of(2 * order), jnp.where(bwd, taps_of(2 * order + 1), 0.0)) * decay
        k = jnp.where(m == 0, k + skip_ref[order:order + 1, :], k)
        for c, kc in enumerate(_lane_chunks(k)):
            for j in range(FILTER_N2):
                o_ref[order, c, j] = kc[j * n1:(j + 1) * n1]


def hyena_filter_taps(w1, b1, w2, b2, w3, b3, freq, w4, skip, *, L, N1):
    C = skip.shape[1]
    fw = w2.shape[0]
    N2 = FFT_N2
    bands = jnp.linspace(1e-4, FILTER_BANDS - 1, FILTER_BANDS, dtype=F32).reshape(FILTER_BANDS, 1)
    deltas = jnp.abs(jnp.linspace(math.log(DECAY_TARGET) / SLOW_DECAY_PCT,
                                  math.log(DECAY_TARGET) / FAST_DECAY_PCT, C, dtype=F32)).reshape(1, C)
    col = lambda a: a.reshape(fw, 1)
    args = (bands, deltas, w1[0:1].T, w1[1:1 + FILTER_BANDS].T, w1[1 + FILTER_BANDS:].T,
            col(b1), w2.T, col(b2), w3.T, col(b3), col(freq), w4, skip)
    small = lambda a: pl.BlockSpec(a.shape, lambda i: (0,) * a.ndim)
    return pl.pallas_call(
        functools.partial(_filter_kernel, L=L, N=N1 * N2, C=C),
        grid=(N2 // FILTER_N2,),
        in_specs=[small(a) for a in args],
        out_specs=pl.BlockSpec((2, C // LANES, FILTER_N2, N1, LANES), lambda i: (0, 0, i, 0, 0)),
        out_shape=jax.ShapeDtypeStruct((2, C // LANES, N2, N1, LANES), F32),
        compiler_params=_cparams(1),
        name="hyena_filter_taps",
    )(*args)


def _real_form(m):
    return np.block([[m.real, -m.imag], [m.imag, m.real]])


def _interleave(n):
    return np.arange(2 * n).reshape(2, n).T.reshape(-1)


@functools.lru_cache(maxsize=None)
def _dft_tables(N1, N2):
    N = N1 * N2
    k1 = np.arange(N1)
    g1 = np.exp(-2j * np.pi * ((k1[:, None] * k1[None, :]) % N1) / N1)
    il1, il2 = _interleave(N1), _interleave(N2)
    stage1 = _real_form(g1)[il1][:, il1]
    stage1_real = _real_form(g1)[il1][:, :N1]
    stage1_inv = _real_form(np.conj(g1))[il1][:, il1]
    n2 = np.arange(N2)
    k = k1[:, None, None] + N1 * n2[None, :, None]
    mk = np.exp(-2j * np.pi * ((k * n2[None, None, :]) % N) / N)
    fwd = np.stack([_real_form(mk[i]) for i in range(N1)])
    stage2 = fwd[:, :, il2]
    stage2_il = fwd[:, il2][:, :, il2]
    stage2_inv = np.stack([_real_form(np.conj(mk[i]).T) for i in range(N1)])[:, il2]
    return tuple(a.astype(BF16) for a in
                 (stage1, stage1_real, stage1_inv, stage2, stage2_il, stage2_inv))


STAGE1_J = 16
STAGE1_CHUNKS = 4


STAGE2_K1 = 8


def _time_spec(jb, cb, n1, lead):
    if lead is None:
        return pl.BlockSpec((None, cb, jb, n1, LANES), lambda p, j, c: (p, c, j, 0, 0))
    return pl.BlockSpec((None, None, cb, jb, n1, LANES), lambda p, j, c: (lead, p, c, j, 0, 0))


def _freq_spec_n2(jb, cb, n1):
    kb = STAGE2_K1
    return pl.BlockSpec((None, cb, n1 // kb, jb * kb, LANES), lambda p, j, c: (p, c, 0, j, 0))


def _load_time(x_ref, j):
    return jnp.concatenate([x_ref[c, j] for c in range(x_ref.shape[0])], axis=1)


def _load_freq(x_ref, j):
    kb = STAGE2_K1
    rows = slice(j * kb, (j + 1) * kb)
    return jnp.concatenate(
        [jnp.concatenate([x_ref[c, k, rows, :] for k in range(x_ref.shape[1])], axis=0)
         for c in range(x_ref.shape[0])], axis=1)


def _store_time(o_ref, j, word):
    for c, w in enumerate(_lane_chunks(word)):
        o_ref[c, j] = w


def _store_freq(o_ref, j, word):
    kb = STAGE2_K1
    for c, w in enumerate(_lane_chunks(word)):
        for k in range(o_ref.shape[1]):
            o_ref[c, k, j * kb:(j + 1) * kb, :] = w[k * kb:(k + 1) * kb]


def _stage1_matmul(m_ref, word, real_input):
    operand = word.astype(BF16) if real_input else _words_to_bf16(word)
    return jnp.dot(m_ref[...], operand, preferred_element_type=F32)


def _stage1_kernel(m_ref, x_ref, o_ref, *, real_input):
    for j in range(STAGE1_J):
        y = _stage1_matmul(m_ref, _load_time(x_ref, j), real_input)
        _store_freq(o_ref, j, _f32_to_words(y))


def dft_stage1(mat, x, lead=None):
    P, chunks, N2, n1x, _ = x.shape[-5:]
    N1 = mat.shape[0] // 2
    assert mat.shape[1] == (n1x if x.dtype == F32 else 2 * n1x)
    jb, cb, kb = STAGE1_J, STAGE1_CHUNKS, STAGE2_K1
    return pl.pallas_call(
        functools.partial(_stage1_kernel, real_input=(x.dtype == F32)),
        grid=(P, N2 // jb, chunks // cb),
        in_specs=[pl.BlockSpec(mat.shape, lambda p, j, c: (0, 0)),
                  _time_spec(jb, cb, n1x, lead)],
        out_specs=_freq_spec_n2(jb, cb, N1),
        out_shape=jax.ShapeDtypeStruct((P, chunks, N1 // kb, N2 * kb, LANES), U32),
        compiler_params=_cparams(3),
        name="dft_stage1",
    )(mat, x)


def _stage1_gate_kernel(m_ref, x_ref, gate_ref, o_ref):
    for j in range(STAGE1_J):
        y = _stage1_matmul(m_ref, _load_freq(x_ref, j), False)
        gate = _words_to_bf16(_load_time(gate_ref, j)).astype(F32)
        _store_time(o_ref, j, _f32_to_words(y * gate))


def dft_stage1_inverse_gate(mat, x, gates, lead):
    P, chunks, n_kb, _, _ = x.shape
    N2, n1g = gates.shape[-3], gates.shape[-2]
    N1 = n_kb * STAGE2_K1
    assert mat.shape == (2 * n1g, 2 * N1)
    jb, cb = STAGE1_J, STAGE1_CHUNKS
    return pl.pallas_call(
        _stage1_gate_kernel,
        grid=(P, N2 // jb, chunks // cb),
        in_specs=[pl.BlockSpec(mat.shape, lambda p, j, c: (0, 0)),
                  _freq_spec_n2(jb, cb, N1),
                  _time_spec(jb, cb, n1g, lead)],
        out_specs=_time_spec(jb, cb, n1g, None),
        out_shape=jax.ShapeDtypeStruct((P, chunks, N2, n1g, LANES), U32),
        compiler_params=_cparams(3),
        name="dft_stage1_inverse_gate",
    )(mat, x, gates)


def _stage1_gate_stage1_kernel(mi_ref, mf_ref, x_ref, gate_ref, o_ref, z_ref):
    for j in range(STAGE1_J):
        y = _stage1_matmul(mi_ref, _load_freq(x_ref, j), False)
        gate = _words_to_bf16(_load_time(gate_ref, j)).astype(F32)
        z_ref[j] = (y * gate).astype(BF16)
    for j in range(STAGE1_J):
        a = jnp.dot(mf_ref[...], z_ref[j], preferred_element_type=F32)
        _store_freq(o_ref, j, _f32_to_words(a))


def dft_stage1_inverse_gate_forward(mat_inv, mat_fwd, x, gates, lead):
    P, chunks, n_kb, _, _ = x.shape
    N2, n1g = gates.shape[-3], gates.shape[-2]
    N1 = n_kb * STAGE2_K1
    assert mat_inv.shape == (2 * n1g, 2 * N1) and mat_fwd.shape == (2 * N1, 2 * n1g)
    jb, cb = STAGE1_J, STAGE1_CHUNKS
    return pl.pallas_call(
        _stage1_gate_stage1_kernel,
        grid=(P, N2 // jb, chunks // cb),
        in_specs=[pl.BlockSpec(mat_inv.shape, lambda p, j, c: (0, 0)),
                  pl.BlockSpec(mat_fwd.shape, lambda p, j, c: (0, 0)),
                  _freq_spec_n2(jb, cb, N1),
                  _time_spec(jb, cb, n1g, lead)],
        out_specs=_freq_spec_n2(jb, cb, N1),
        out_shape=jax.ShapeDtypeStruct(x.shape, U32),
        scratch_shapes=[pltpu.VMEM((jb, 2 * n1g, cb * LANES), BF16)],
        compiler_params=_cparams(3),
        name="dft_stage1_inverse_gate_forward",
    )(mat_inv, mat_fwd, x, gates)


def _freq_spec_k1(chunks, n2):
    kb = STAGE2_K1
    return pl.BlockSpec((None, chunks, None, n2 * kb, LANES), lambda p, k: (p, 0, k, 0, 0))


def _stage2_rows(ref, i):
    return pl.ds(i, ref.shape[1] // STAGE2_K1, stride=STAGE2_K1)


def _stage2_operand(a_ref, i):
    rows = _stage2_rows(a_ref, i)
    word = jnp.concatenate([a_ref[c, rows, :] for c in range(a_ref.shape[0])], axis=1)
    return _words_to_bf16(word)


def _stage2_spectrum_kernel(m_ref, a_ref, o_ref, *, scale):
    for i in range(STAGE2_K1):
        x = jnp.dot(m_ref[i], _stage2_operand(a_ref, i), preferred_element_type=F32) * scale
        o_ref[i] = _f32_to_words(x)


def filter_spectrum(stage2, a, *, N2):
    kb = STAGE2_K1
    _, chunks, n_kb, _, _ = a.shape
    N1 = n_kb * kb
    C = chunks * LANES
    return pl.pallas_call(
        functools.partial(_stage2_spectrum_kernel, scale=1.0 / (N1 * N2)),
        grid=(2, n_kb),
        in_specs=[pl.BlockSpec((kb, 2 * N2, 2 * N2), lambda o, k: (k, 0, 0)),
                  _freq_spec_k1(chunks, N2)],
        out_specs=pl.BlockSpec((None, kb, N2, C), lambda o, k: (o, k, 0, 0)),
        out_shape=jax.ShapeDtypeStruct((2, N1, N2, C), U32),
        compiler_params=_cparams(2),
        name="filter_spectrum",
    )(stage2, a)


def _stage2_conv_kernel(mf_ref, mi_ref, a_ref, k_ref, o_ref, y_ref):
    n2 = k_ref.shape[1]
    for i in range(STAGE2_K1):
        x = jnp.dot(mf_ref[i], _stage2_operand(a_ref, i), preferred_element_type=F32)
        xr, xi = x[:n2], x[n2:]
        kr, ki = _unpack(k_ref[i])
        y_ref[i] = jnp.concatenate([xr * kr - xi * ki, xr * ki + xi * kr], axis=0).astype(BF16)
    for i in range(STAGE2_K1):
        b = jnp.dot(mi_ref[i], y_ref[i], preferred_element_type=F32)
        rows = _stage2_rows(o_ref, i)
        for c, word in enumerate(_lane_chunks(_f32_to_words(b))):
            o_ref[c, rows, :] = word


def dft_stage2_conv(stage2, stage2_inv, a, kspec, order):
    kb = STAGE2_K1
    P, chunks, n_kb, _, _ = a.shape
    N2, C = kspec.shape[2], kspec.shape[3]
    seq = pl.BlockSpec((None, chunks, None, N2 * kb, LANES), lambda k, p: (p, 0, k, 0, 0))
    return pl.pallas_call(
        _stage2_conv_kernel,
        grid=(n_kb, P),
        in_specs=[pl.BlockSpec((kb, 2 * N2, 2 * N2), lambda k, p: (k, 0, 0)),
                  pl.BlockSpec((kb, 2 * N2, 2 * N2), lambda k, p: (k, 0, 0)),
                  seq,
                  pl.BlockSpec((None, kb, N2, C), lambda k, p: (order, k, 0, 0))],
        out_specs=seq,
        out_shape=jax.ShapeDtypeStruct(a.shape, U32),
        scratch_shapes=[pltpu.VMEM((kb, 2 * N2, C), BF16)],
        compiler_params=_cparams(2),
        name="dft_stage2_conv",
    )(stage2, stage2_inv, a, kspec)


def hyena_mixer(proj, conv_w, conv_b, w1, b1, w2, b2, w3, b3, freq, w4, skip, *, B, L, col0):
    C = skip.shape[1]
    N1, N2 = _fft_factor(L), FFT_N2
    stage1, stage1_real, stage1_inv, stage2, stage2_il, stage2_inv = _dft_tables(N1, N2)

    taps = hyena_filter_taps(w1, b1, w2, b2, w3, b3, freq, w4, skip, L=L, N1=N1)
    kspec = filter_spectrum(stage2_il, dft_stage1(stage1_real, taps), N2=N2)

    streams = hyena_prep(proj, conv_w, conv_b, B=B, L=L, col0=col0, width=C)
    valid = 2 * _valid_n1(L)
    a = dft_stage1(stage1[:, :valid], streams, lead=0)
    b = dft_stage2_conv(stage2, stage2_inv, a, kspec, 0)
    a = dft_stage1_inverse_gate_forward(stage1_inv[:valid], stage1[:, :valid], b, streams, 1)
    b = dft_stage2_conv(stage2, stage2_inv, a, kspec, 1)
    z = dft_stage1_inverse_gate(stage1_inv[:valid], b, streams, 2)
    return unpack_pairs(z, L=L)


def _trunk(x, meta_tokens, norm1_g, in_proj, pool_w, pool_scale, hy_conv_w, hy_conv_b,
           flt_w1, flt_b1, flt_w2, flt_b2, flt_w3, flt_b3, flt_freq, flt_w4, hy_skip,
           out_proj, norm2_g, w_gate, w_up, w_down, final_g):
    B, S, D = x.shape
    assert meta_tokens.shape[0] == N_META
    L = S + N_META
    depth = in_proj.shape[0]
    pool_width = pool_scale.shape[1]
    h, xn = embed_norm(x, meta_tokens.astype(x.dtype), norm1_g[0])
    for l in range(depth):
        proj = matmul_resident(xn, in_proj, l).reshape(B, L, in_proj.shape[2])
        a = pool_mixer(proj, pool_w[l], pool_scale[l], B=B, L=L)
        b = hyena_mixer(proj, hy_conv_w[l], hy_conv_b[l], flt_w1[l], flt_b1[l], flt_w2[l],
                        flt_b2[l], flt_w3[l], flt_b3[l], flt_freq[l], flt_w4[l], hy_skip[l],
                        B=B, L=L, col0=pool_width)
        h, xn = mixer_residual_norm(a, b, out_proj, l, h, norm2_g[l], B=B, L=L)
        act = swiglu(xn, w_gate, w_up, l, tn=512)
        if l + 1 < depth:
            h, xn = matmul_residual_norm(act, w_down, l, h, norm1_g[l + 1])
    return matmul_residual_final(act, w_down, depth - 1, h, final_g, B=B, L=L, skip=N_META)


def kernel(x_prompt, x_sample, meta_tokens, norm1_g, in_proj, pool_w, pool_scale, hy_conv_w, hy_conv_b, flt_w1, flt_b1, flt_w2, flt_b2, flt_w3, flt_b3, flt_freq, flt_w4, hy_skip, out_proj, norm2_g, w_gate, w_up, w_down, final_g):
    params = (meta_tokens, norm1_g, in_proj.astype(BF16), pool_w.astype(BF16), pool_scale,
              hy_conv_w, hy_conv_b, flt_w1, flt_b1, flt_w2, flt_b2, flt_w3, flt_b3, flt_freq,
              flt_w4, hy_skip, out_proj.astype(BF16), norm2_g, w_gate.astype(BF16),
              w_up.astype(BF16), w_down.astype(BF16), final_g)
    return (_trunk(x_prompt, *params), _trunk(x_sample, *params))
```
